```python
import jax
import jax.numpy as jnp
from jax import lax
import numpy as np

D_MODEL = 1024
BATCH = 8
SEQ = 2048
DEPTH = 2
DEC_BATCH = 8
DEC_SEQ = 16
PAST_LEN = 4096

CHUNK = 64
N_MIXERS = 2
N_GLA = (DEPTH + 1) // 2
N_SGU = DEPTH // 2
GLA_HEADS = 4
GLA_KEY = D_MODEL // 2
GLA_VAL = D_MODEL
GLA_DK = GLA_KEY // GLA_HEADS
GLA_DV = GLA_VAL // GLA_HEADS
GLA_GATE_RANK = 16
GLA_TAU = 16.0
GLA_IN = 2 * GLA_KEY + 2 * GLA_VAL + GLA_GATE_RANK
SGU_CHUNK = 128
SGU_GROUPS = 4
SGU_HALF = 3 * D_MODEL
SGU_GROUP_DIM = SGU_HALF // SGU_GROUPS
D_FF = 4 * D_MODEL
NORM_EPS = 1e-6
LN_EPS = 1e-5

kernel_name = "gla_sgu_macaron_stream_step"


def rms_norm(x, g):
    xf = x.astype(jnp.float32)
    y = xf * lax.rsqrt(jnp.mean(xf * xf, axis=-1, keepdims=True) + NORM_EPS)
    return (y * g.astype(jnp.float32)).astype(x.dtype)


def layer_norm(x, g, b):
    xf = x.astype(jnp.float32)
    mu = jnp.mean(xf, axis=-1, keepdims=True)
    xc = xf - mu
    y = xc * lax.rsqrt(jnp.mean(xc * xc, axis=-1, keepdims=True) + LN_EPS)
    return (y * g.astype(jnp.float32) + b.astype(jnp.float32)).astype(x.dtype)


def swiglu_ffn(x, w_gate, w_up, w_down):
    return (jax.nn.silu(x @ w_gate) * (x @ w_up)) @ w_down


def gla_chunk(S, chunk):
    q, k, v, g = chunk
    b = jnp.cumsum(g, axis=1)
    b_last = b[:, -1]
    qf = q.astype(jnp.float32)
    kf = k.astype(jnp.float32)
    vf = v.astype(jnp.float32)
    qg = qf * jnp.exp(b)
    kg = kf * jnp.exp(-b)
    kd = kf * jnp.exp(b_last[:, None] - b)
    L = q.shape[1]
    mask = jnp.tril(jnp.ones((L, L), dtype=bool))
    A = jnp.where(mask, jnp.einsum('bthd,bshd->bhts', qg, kg), 0.0)
    o = jnp.einsum('bhts,bshv->bthv', A, vf) + jnp.einsum('bthd,bhdv->bthv', qg, S)
    S_new = jnp.exp(b_last)[..., None] * S + jnp.einsum('bshd,bshv->bhdv', kd, vf)
    return S_new, o


def gla_mixer(x, S0, w_in, w_gate2, b_gate, g_norm, w_out):
    B, T, _ = x.shape
    L = min(T, CHUNK)
    N = T // L
    h = x @ w_in
    q, k, v, r, zg = jnp.split(
        h, [GLA_KEY, 2 * GLA_KEY, 2 * GLA_KEY + GLA_VAL, 2 * GLA_KEY + 2 * GLA_VAL], axis=-1)
    g = jax.nn.log_sigmoid((zg @ w_gate2 + b_gate).astype(jnp.float32)) / GLA_TAU
    q = q * (GLA_DK ** -0.5)

    def heads(t, d):
        return jnp.moveaxis(t.reshape(B, N, L, GLA_HEADS, d), 1, 0)

    S, o = lax.scan(gla_chunk, S0.astype(jnp.float32),
                    (heads(q, GLA_DK), heads(k, GLA_DK), heads(v, GLA_DV), heads(g, GLA_DK)))
    o = jnp.moveaxis(o, 0, 1).reshape(B, T, GLA_HEADS, GLA_DV)
    o = rms_norm(o, g_norm).astype(x.dtype).reshape(B, T, GLA_VAL)
    y = (o * jax.nn.silu(r)) @ w_out
    return y, S


def sgu_mixer(x, w_in, b_in, ln_g, ln_b, w_s, b_s, w_out):
    B, T, _ = x.shape
    L = min(T, SGU_CHUNK)
    N = T // L
    z = jax.nn.gelu(x @ w_in + b_in, approximate=False)
    u, v = jnp.split(z, 2, axis=-1)
    v = layer_norm(v, ln_g, ln_b)
    w = jnp.tril(w_s[:, :L, :L])
    vg = v.reshape(B, N, L, SGU_GROUPS, SGU_GROUP_DIM)
    mixed = jnp.einsum('gts,bnsgc->bntgc', w, vg) + b_s[:, :L].T[:, :, None]
    y = (u * mixed.reshape(B, T, SGU_HALF)) @ w_out
    return y, v


def setup_inputs(seed: int = 0) -> dict:
    key = jax.random.key(seed)
    ks = jax.random.split(key, 24)
    D = D_MODEL

    def nrm(k, shape, scale):
        return jax.random.normal(k, shape, jnp.float32) * scale

    return {
        "x_prompt": nrm(ks[0], (BATCH, SEQ, D), 1.0),
        "x_sample": nrm(ks[1], (DEC_BATCH, DEC_SEQ, D), 1.0),
        "state_gla": nrm(ks[2], (N_GLA, DEC_BATCH, GLA_HEADS, GLA_DK, GLA_DV), 1.0),
        "norm_pre": 1.0 + nrm(ks[3], (DEPTH, 3, D), 0.05),
        "norm_post": 1.0 + nrm(ks[4], (DEPTH, 3, D), 0.05),
        "ffn_w_gate": nrm(ks[5], (DEPTH, 2, D, D_FF), D ** -0.5),
        "ffn_w_up": nrm(ks[6], (DEPTH, 2, D, D_FF), D ** -0.5),
        "ffn_w_down": nrm(ks[7], (DEPTH, 2, D_FF, D), D_FF ** -0.5),
        "gla_w_in": nrm(ks[8], (N_GLA, D, GLA_IN), D ** -0.5),
        "gla_w_gate2": nrm(ks[9], (N_GLA, GLA_GATE_RANK, GLA_KEY), GLA_GATE_RANK ** -0.5),
        "gla_b_gate": nrm(ks[10], (N_GLA, GLA_KEY), 0.1),
        "gla_norm": 1.0 + nrm(ks[11], (N_GLA, GLA_DV), 0.05),
        "gla_w_out": nrm(ks[12], (N_GLA, GLA_VAL, D), GLA_VAL ** -0.5),
        "sgu_w_in": nrm(ks[13], (N_SGU, D, 2 * SGU_HALF), D ** -0.5),
        "sgu_b_in": nrm(ks[14], (N_SGU, 2 * SGU_HALF), 0.02),
        "sgu_ln_g": 1.0 + nrm(ks[15], (N_SGU, SGU_HALF), 0.05),
        "sgu_ln_b": nrm(ks[16], (N_SGU, SGU_HALF), 0.02),
        "sgu_w_s": nrm(ks[17], (N_SGU, SGU_GROUPS, SGU_CHUNK, SGU_CHUNK), SGU_CHUNK ** -0.5),
        "sgu_b_s": 1.0 + nrm(ks[18], (N_SGU, SGU_GROUPS, SGU_CHUNK), 0.05),
        "sgu_w_out": nrm(ks[19], (N_SGU, SGU_HALF, D), SGU_HALF ** -0.5),
    }


def reference(x_prompt, x_sample, state_gla, norm_pre, norm_post, ffn_w_gate, ffn_w_up,
              ffn_w_down, gla_w_in, gla_w_gate2, gla_b_gate, gla_norm, gla_w_out,
              sgu_w_in, sgu_b_in, sgu_ln_g, sgu_ln_b, sgu_w_s, sgu_b_s, sgu_w_out):

    def block(x, s0, i):
        j = i // N_MIXERS
        h = swiglu_ffn(rms_norm(x, norm_pre[i, 0]), ffn_w_gate[i, 0], ffn_w_up[i, 0], ffn_w_down[i, 0])
        x = x + 0.5 * rms_norm(h, norm_post[i, 0])
        hn = rms_norm(x, norm_pre[i, 1])
        if i % N_MIXERS == 0:
            h, st = gla_mixer(hn, s0, gla_w_in[j], gla_w_gate2[j], gla_b_gate[j], gla_norm[j], gla_w_out[j])
        else:
            h, st = sgu_mixer(hn, sgu_w_in[j], sgu_b_in[j], sgu_ln_g[j], sgu_ln_b[j],
                              sgu_w_s[j], sgu_b_s[j], sgu_w_out[j])
        x = x + rms_norm(h, norm_post[i, 1])
        h = swiglu_ffn(rms_norm(x, norm_pre[i, 2]), ffn_w_gate[i, 1], ffn_w_up[i, 1], ffn_w_down[i, 1])
        x = x + 0.5 * rms_norm(h, norm_post[i, 2])
        return x, st

    xp, xs = x_prompt, x_sample
    gla_p, gla_s, sgu_s = [], [], []
    for i in range(DEPTH):
        j = i // N_MIXERS
        if i % N_MIXERS == 0:
            s0_p = jnp.zeros((xp.shape[0], GLA_HEADS, GLA_DK, GLA_DV), jnp.float32)
            xp, sp = block(xp, s0_p, i)
            xs, ss = block(xs, state_gla[j], i)
            gla_p.append(sp.astype(x_prompt.dtype))
            gla_s.append(ss.astype(x_sample.dtype))
        else:
            xp, _ = block(xp, None, i)
            xs, vs = block(xs, None, i)
            sgu_s.append(vs)

    gla_state_prompt = jnp.stack(gla_p)
    gla_state_sample = jnp.stack(gla_s)
    sgu_v_sample = jnp.stack(sgu_s)
    return (xp, xs, gla_state_prompt, gla_state_sample, sgu_v_sample)
```

```python
import functools

import jax
import jax.numpy as jnp
from jax import lax
from jax.experimental import pallas as pl
from jax.experimental.pallas import tpu as pltpu

F32 = jnp.float32
BF16 = jnp.bfloat16

NORM_EPS = 1e-6
LN_EPS = 1e-5
GLA_HEADS = 4
GLA_TAU = 16.0
GLA_CHUNK = 64
SGU_CHUNK = 128
SGU_GROUPS = 4

VMEM_LIMIT_BYTES = 56 * 1024 * 1024
TOKEN_TILE = 512
FFN_COL_CHUNK = 512


def _dot(a, b):
    return jnp.dot(a, b, preferred_element_type=F32)


def _rms(x, g, eps=NORM_EPS):
    return x * lax.rsqrt(jnp.mean(x * x, axis=-1, keepdims=True) + eps) * g


def _silu(x):
    return x * jax.nn.sigmoid(x)


def _gelu(x):
    return 0.5 * x * (1.0 + lax.erf(x * (0.5 ** 0.5)))


def _resident(shape):
    zeros = (0,) * len(shape)
    return pl.BlockSpec(shape, lambda *_: zeros, pipeline_mode=pl.Buffered(1))


def _ffn_kernel(x_ref, pre_ref, post_ref, wg_ref, wu_ref, wd_ref, o_ref):
    x = x_ref[...]
    xn = _rms(x, pre_ref[...]).astype(BF16)
    d_ff = wg_ref.shape[1]
    acc = jnp.zeros(x.shape, F32)
    for c in range(d_ff // FFN_COL_CHUNK):
        cols = slice(c * FFN_COL_CHUNK, (c + 1) * FFN_COL_CHUNK)
        h = _silu(_dot(xn, wg_ref[:, cols])) * _dot(xn, wu_ref[:, cols])
        acc = acc + _dot(h.astype(BF16), wd_ref[cols, :])
    o_ref[...] = x + 0.5 * _rms(acc, post_ref[...])


def _ffn(x, pre, post, wg, wu, wd):
    m, d = x.shape
    tm = min(TOKEN_TILE, m)
    row = pl.BlockSpec((tm, d), lambda i: (i, 0))
    return pl.pallas_call(
        _ffn_kernel,
        grid=(m // tm,),
        in_specs=[row, _resident((1, d)), _resident((1, d)),
                  _resident(wg.shape), _resident(wu.shape), _resident(wd.shape)],
        out_specs=row,
        out_shape=jax.ShapeDtypeStruct((m, d), F32),
        compiler_params=pltpu.CompilerParams(
            dimension_semantics=("arbitrary",), vmem_limit_bytes=VMEM_LIMIT_BYTES),
        name="ffn",
    )(x, pre.reshape(1, d), post.reshape(1, d), wg, wu, wd)


def _gla_kernel(x_ref, s0_ref, pre_ref, post_ref, win_ref, wg2_ref, bg_ref, gn_ref, wout_ref,
                y_ref, s_ref, q_s, k_s, v_s, g_s, o_s, xn_s, *, nb, nc, L):
    heads = GLA_HEADS
    key = wg2_ref.shape[1]
    val = wout_ref.shape[0]
    dk, dv = key // heads, val // heads

    @pl.when(pl.program_id(1) == 0)
    def _():
        s_ref[...] = s0_ref[...]

    xn = _rms(x_ref[...], pre_ref[...]).astype(BF16)
    xn_s[...] = xn
    q_s[...] = _dot(xn, win_ref[:, 0:key]) * (dk ** -0.5)
    k_s[...] = _dot(xn, win_ref[:, key:2 * key])
    v_s[...] = _dot(xn, win_ref[:, 2 * key:2 * key + val]).astype(BF16)
    zg = _dot(xn, win_ref[:, 2 * key + 2 * val:])
    z = _dot(zg.astype(BF16), wg2_ref[...]) + bg_ref[...]
    g_s[...] = jax.nn.log_sigmoid(z) / GLA_TAU

    row = lax.broadcasted_iota(jnp.int32, (L, L), 0)
    col = lax.broadcasted_iota(jnp.int32, (L, L), 1)
    causal = row >= col
    tri = jnp.where(causal, 1.0, 0.0).astype(BF16)

    def chunk(i, carry):
        r0 = pl.multiple_of(i * L, L)
        bi = i // nc if nb > 1 else 0
        rows = pl.ds(r0, L)
        gc = g_s[rows, :]
        g1 = gc.astype(BF16)
        rem = gc - g1.astype(F32)
        g2 = rem.astype(BF16)
        g3 = (rem - g2.astype(F32)).astype(BF16)
        b = _dot(tri, g1) + _dot(tri, g2) + _dot(tri, g3)
        b_last = b[L - 1:L, :]
        qc = q_s[rows, :]
        kc = k_s[rows, :]
        qg = (qc * jnp.exp(b)).astype(BF16)
        kg = (kc * jnp.exp(-b)).astype(BF16)
        kd = (kc * jnp.exp(b_last - b)).astype(BF16)
        e_last = jnp.exp(b_last)
        for h in range(heads):
            ks = slice(h * dk, (h + 1) * dk)
            vs = slice(h * dv, (h + 1) * dv)
            s_h = s_ref[bi, h]
            vh = v_s[rows, vs]
            a = lax.dot_general(qg[:, ks], kg[:, ks], (((1,), (1,)), ((), ())),
                                preferred_element_type=F32)
            a = jnp.where(causal, a, 0.0).astype(BF16)
            o_s[rows, vs] = _dot(a, vh) + _dot(qg[:, ks], s_h.astype(BF16))
            decay = jnp.transpose(jnp.broadcast_to(e_last[:, ks], (dk, dk)))
            upd = lax.dot_general(kd[:, ks], vh, (((0,), (0,)), ((), ())),
                                  preferred_element_type=F32)
            s_ref[bi, h] = s_h * jnp.tile(decay, (1, dv // dk)) + upd
        return carry

    lax.fori_loop(0, nb * nc, chunk, 0)

    r = _dot(xn_s[...], win_ref[:, 2 * key + val:2 * key + 2 * val])
    gn = gn_ref[...]
    o_n = jnp.concatenate(
        [_rms(o_s[:, h * dv:(h + 1) * dv], gn) for h in range(heads)], axis=1)
    y = _dot((o_n * _silu(r)).astype(BF16), wout_ref[...])
    y_ref[...] = x_ref[...] + _rms(y, post_ref[...])


def _gla(x, s0, pre, post, w_in, w_gate2, b_gate, g_norm, w_out, *, batch, seq, chunk):
    m, d = x.shape
    key = w_gate2.shape[1]
    val = w_out.shape[0]
    if seq >= TOKEN_TILE:
        nb, tt = 1, TOKEN_TILE
    else:
        nb, tt = batch, seq
    assert seq % tt == 0 and tt % chunk == 0 and batch % nb == 0
    nt = seq // tt
    r = nb * tt
    row = pl.BlockSpec((r, d), lambda b, t: (b * nt + t, 0))
    state = pl.BlockSpec((nb,) + s0.shape[1:], lambda b, t: (b, 0, 0, 0))
    kern = functools.partial(_gla_kernel, nb=nb, nc=tt // chunk, L=chunk)
    return pl.pallas_call(
        kern,
        grid=(batch // nb, nt),
        in_specs=[row, state, _resident((1, d)), _resident((1, d)), _resident(w_in.shape),
                  _resident(w_gate2.shape), _resident((1, key)), _resident((1, g_norm.shape[0])),
                  _resident(w_out.shape)],
        out_specs=[row, state],
        out_shape=[jax.ShapeDtypeStruct((m, d), F32), jax.ShapeDtypeStruct(s0.shape, F32)],
        scratch_shapes=[pltpu.VMEM((r, key), F32), pltpu.VMEM((r, key), F32),
                        pltpu.VMEM((r, val), BF16), pltpu.VMEM((r, key), F32),
                        pltpu.VMEM((r, val), F32), pltpu.VMEM((r, d), BF16)],
        compiler_params=pltpu.CompilerParams(
            dimension_semantics=("arbitrary", "arbitrary"), vmem_limit_bytes=VMEM_LIMIT_BYTES),
        name="gla",
    )(x, s0, pre.reshape(1, d), post.reshape(1, d), w_in, w_gate2, b_gate.reshape(1, key),
      g_norm.reshape(1, -1), w_out)


def _sgu_kernel(x_ref, pre_ref, post_ref, win_ref, bin_ref, lng_ref, lnb_ref, wmix_ref, bmix_ref,
                wout_ref, y_ref, *rest, L, emit_v):
    if emit_v:
        v_ref, v_s = rest
    else:
        (v_s,) = rest
    groups = SGU_GROUPS
    half = wout_ref.shape[0]
    gd = half // groups
    r = x_ref.shape[0]
    win = wmix_ref.shape[1]

    x = x_ref[...]
    xn = _rms(x, pre_ref[...]).astype(BF16)

    s1 = jnp.zeros((r, 1), F32)
    for g in range(groups):
        seg = slice(half + g * gd, half + (g + 1) * gd)
        vg = _gelu(_dot(xn, win_ref[:, seg]) + bin_ref[:, seg])
        v_s[:, g * gd:(g + 1) * gd] = vg
        s1 = s1 + jnp.sum(vg, axis=-1, keepdims=True)
    mu = s1 / half
    s2 = jnp.zeros((r, 1), F32)
    for g in range(groups):
        xc = v_s[:, g * gd:(g + 1) * gd] - mu
        s2 = s2 + jnp.sum(xc * xc, axis=-1, keepdims=True)
    rstd = lax.rsqrt(s2 / half + LN_EPS)

    row = lax.broadcasted_iota(jnp.int32, (win, win), 0)
    col = lax.broadcasted_iota(jnp.int32, (win, win), 1)
    mask = row >= col
    if L < win:
        mask = jnp.logical_and(mask, row // L == col // L)

    acc = jnp.zeros(x.shape, F32)
    for g in range(groups):
        seg = slice(g * gd, (g + 1) * gd)
        vn = (v_s[:, seg] - mu) * rstd * lng_ref[:, seg] + lnb_ref[:, seg]
        if emit_v:
            v_ref[:, seg] = vn
        vnb = vn.astype(BF16)
        wm = jnp.where(mask, wmix_ref[g], 0.0).astype(BF16)
        bcol = bmix_ref[:, g:g + 1]
        mixed = jnp.concatenate(
            [_dot(wm, vnb[c * win:(c + 1) * win, :]) + bcol for c in range(r // win)], axis=0)
        u = _gelu(_dot(xn, win_ref[:, seg]) + bin_ref[:, seg])
        acc = acc + _dot((u * mixed).astype(BF16), wout_ref[seg, :])
    y_ref[...] = x + _rms(acc, post_ref[...])


def _sgu(x, pre, post, w_in, b_in, ln_g, ln_b, w_mix, b_mix, w_out, *, chunk, emit_v):
    m, d = x.shape
    half = w_out.shape[0]
    r = min(TOKEN_TILE, m)
    row = pl.BlockSpec((r, d), lambda i: (i, 0))
    out_specs = [row]
    out_shape = [jax.ShapeDtypeStruct((m, d), F32)]
    if emit_v:
        out_specs.append(pl.BlockSpec((r, half), lambda i: (i, 0)))
        out_shape.append(jax.ShapeDtypeStruct((m, half), F32))
    kern = functools.partial(_sgu_kernel, L=chunk, emit_v=emit_v)
    return pl.pallas_call(
        kern,
        grid=(m // r,),
        in_specs=[row, _resident((1, d)), _resident((1, d)), _resident(w_in.shape),
                  _resident((1, 2 * half)), _resident((1, half)), _resident((1, half)),
                  _resident(w_mix.shape), _resident(b_mix.shape), _resident(w_out.shape)],
        out_specs=out_specs,
        out_shape=out_shape,
        scratch_shapes=[pltpu.VMEM((r, half), F32)],
        compiler_params=pltpu.CompilerParams(
            dimension_semantics=("arbitrary",), vmem_limit_bytes=VMEM_LIMIT_BYTES),
        name="sgu",
    )(x, pre.reshape(1, d), post.reshape(1, d), w_in, b_in.reshape(1, -1), ln_g.reshape(1, -1),
      ln_b.reshape(1, -1), w_mix, b_mix, w_out)


def kernel(x_prompt, x_sample, state_gla, norm_pre, norm_post, ffn_w_gate, ffn_w_up, ffn_w_down,
           gla_w_in, gla_w_gate2, gla_b_gate, gla_norm, gla_w_out, sgu_w_in, sgu_b_in, sgu_ln_g,
           sgu_ln_b, sgu_w_s, sgu_b_s, sgu_w_out):
    batch, seq, d = x_prompt.shape
    dbatch, dseq, _ = x_sample.shape
    depth = norm_pre.shape[0]
    xp = x_prompt.reshape(batch * seq, d)
    xs = x_sample.reshape(dbatch * dseq, d)
    gla_p, gla_s, sgu_s = [], [], []

    for i in range(depth):
        j = i // 2

        def ffn_pair(xp, xs, k, n):
            wg = ffn_w_gate[i, k].astype(BF16)
            wu = ffn_w_up[i, k].astype(BF16)
            wd = ffn_w_down[i, k].astype(BF16)
            return (_ffn(xp, norm_pre[i, n], norm_post[i, n], wg, wu, wd),
                    _ffn(xs, norm_pre[i, n], norm_post[i, n], wg, wu, wd))

        xp, xs = ffn_pair(xp, xs, 0, 0)

        if i % 2 == 0:
            w_in = gla_w_in[j].astype(BF16)
            w_g2 = gla_w_gate2[j].astype(BF16)
            w_out = gla_w_out[j].astype(BF16)
            args = (norm_pre[i, 1], norm_post[i, 1], w_in, w_g2, gla_b_gate[j], gla_norm[j], w_out)
            zero_state = jnp.zeros((batch,) + state_gla.shape[2:], F32)
            xp, sp = _gla(xp, zero_state, *args, batch=batch, seq=seq, chunk=min(seq, GLA_CHUNK))
            xs, ss = _gla(xs, state_gla[j], *args, batch=dbatch, seq=dseq,
                          chunk=min(dseq, GLA_CHUNK))
            gla_p.append(sp)
            gla_s.append(ss)
        else:
            w_in = sgu_w_in[j].astype(BF16)
            w_out = sgu_w_out[j].astype(BF16)
            args = (norm_pre[i, 1], norm_post[i, 1], w_in, sgu_b_in[j], sgu_ln_g[j], sgu_ln_b[j])
            lp = min(seq, SGU_CHUNK)
            assert lp == SGU_CHUNK
            (xp,) = _sgu(xp, *args, sgu_w_s[j], sgu_b_s[j].T, w_out, chunk=lp, emit_v=False)
            ls = min(dseq, SGU_CHUNK)
            reps = SGU_CHUNK // ls
            assert (dbatch * dseq) % SGU_CHUNK == 0 and SGU_CHUNK % ls == 0
            w_mix = jnp.tile(sgu_w_s[j][:, :ls, :ls], (1, reps, reps))
            b_mix = jnp.tile(sgu_b_s[j][:, :ls].T, (reps, 1))
            xs, vs = _sgu(xs, *args, w_mix, b_mix, w_out, chunk=ls, emit_v=True)
            sgu_s.append(vs.reshape(dbatch, dseq, -1))

        xp, xs = ffn_pair(xp, xs, 1, 2)

    return (xp.reshape(batch, seq, d), xs.reshape(dbatch, dseq, d),
            jnp.stack(gla_p), jnp.stack(gla_s), jnp.stack(sgu_s))
```

```python
import functools

import jax
import jax.numpy as jnp
from jax import lax
from jax.experimental import pallas as pl
from jax.experimental.pallas import tpu as pltpu

F32 = jnp.float32
BF16 = jnp.bfloat16

NORM_EPS = 1e-6
LN_EPS = 1e-5
GLA_HEADS = 4
GLA_TAU = 16.0
GLA_CHUNK = 64
SGU_CHUNK = 128
SGU_GROUPS = 4

VMEM_LIMIT_BYTES = 56 * 1024 * 1024
TOKEN_TILE = 512
FFN_COL_CHUNK = 512


def _dot(a, b):
    return jnp.dot(a, b, preferred_element_type=F32)


def _rms(x, g, eps=NORM_EPS):
    return x * lax.rsqrt(jnp.mean(x * x, axis=-1, keepdims=True) + eps) * g


def _silu(x):
    return x * jax.nn.sigmoid(x)


def _gelu(x):
    return 0.5 * x * (1.0 + lax.erf(x * (0.5 ** 0.5)))


def _resident(shape):
    zeros = (0,) * len(shape)
    return pl.BlockSpec(shape, lambda *_: zeros, pipeline_mode=pl.Buffered(1))


def _prompt_rows(tile, width, n_prompt):
    return pl.BlockSpec((tile, width), lambda i: (jnp.minimum(i, n_prompt - 1), 0))


def _two_streams(body_prompt, body_sample, n_prompt):
    step = pl.program_id(0)
    pl.when(step < n_prompt)(body_prompt)
    pl.when(step == n_prompt)(body_sample)


_PARAMS = pltpu.CompilerParams(dimension_semantics=("arbitrary",),
                               vmem_limit_bytes=VMEM_LIMIT_BYTES)


def _ffn_tile(x_ref, o_ref, pre_ref, post_ref, wg_ref, wu_ref, wd_ref):
    x = x_ref[...]
    xn = _rms(x, pre_ref[...]).astype(BF16)
    d_ff = wg_ref.shape[1]
    acc = jnp.zeros(x.shape, F32)
    for c in range(d_ff // FFN_COL_CHUNK):
        cols = slice(c * FFN_COL_CHUNK, (c + 1) * FFN_COL_CHUNK)
        h = _silu(_dot(xn, wg_ref[:, cols])) * _dot(xn, wu_ref[:, cols])
        acc = acc + _dot(h.astype(BF16), wd_ref[cols, :])
    o_ref[...] = x + 0.5 * _rms(acc, post_ref[...])


def _ffn_kernel(xp_ref, xs_ref, pre_ref, post_ref, wg_ref, wu_ref, wd_ref, op_ref, os_ref,
                *, n_prompt):
    weights = (pre_ref, post_ref, wg_ref, wu_ref, wd_ref)
    _two_streams(functools.partial(_ffn_tile, xp_ref, op_ref, *weights),
                 functools.partial(_ffn_tile, xs_ref, os_ref, *weights), n_prompt)


def _ffn(xp, xs, pre, post, wg, wu, wd):
    mp, d = xp.shape
    ms = xs.shape[0]
    n_prompt = mp // TOKEN_TILE
    rows_p = _prompt_rows(TOKEN_TILE, d, n_prompt)
    rows_s = _resident((ms, d))
    return pl.pallas_call(
        functools.partial(_ffn_kernel, n_prompt=n_prompt),
        grid=(n_prompt + 1,),
        in_specs=[rows_p, rows_s, _resident((1, d)), _resident((1, d)),
                  _resident(wg.shape), _resident(wu.shape), _resident(wd.shape)],
        out_specs=[rows_p, pl.BlockSpec((ms, d), lambda i: (0, 0))],
        out_shape=[jax.ShapeDtypeStruct((mp, d), F32), jax.ShapeDtypeStruct((ms, d), F32)],
        compiler_params=_PARAMS,
        name="ffn",
    )(xp, xs, pre.reshape(1, d), post.reshape(1, d), wg, wu, wd)


def _gla_tile(x_ref, y_ref, s_ref, pre_ref, post_ref, win_ref, wg2_ref, bg_ref, gn_ref, wout_ref,
              q_s, k_s, v_s, g_s, o_s, xn_s, qg_s, kg_s, kdt_s, el_s, a_s, u_s, *, nb, nc, L):
    heads = GLA_HEADS
    key = wg2_ref.shape[1]
    val = wout_ref.shape[0]
    dk, dv = key // heads, val // heads
    r = nb * nc * L
    blk = slice(0, r)

    xn = _rms(x_ref[...], pre_ref[...]).astype(BF16)
    xn_s[blk, :] = xn
    q_s[blk, :] = _dot(xn, win_ref[:, 0:key]) * (dk ** -0.5)
    k_s[blk, :] = _dot(xn, win_ref[:, key:2 * key])
    v_s[blk, :] = _dot(xn, win_ref[:, 2 * key:2 * key + val]).astype(BF16)
    zg = _dot(xn, win_ref[:, 2 * key + 2 * val:])
    z = _dot(zg.astype(BF16), wg2_ref[...]) + bg_ref[...]
    g_s[blk, :] = jax.nn.log_sigmoid(z) / GLA_TAU

    row = lax.broadcasted_iota(jnp.int32, (L, L), 0)
    col = lax.broadcasted_iota(jnp.int32, (L, L), 1)
    causal = row >= col
    tri = jnp.where(causal, 1.0, 0.0).astype(BF16)

    chunks = range(nb * nc)
    rows = [slice(i * L, (i + 1) * L) for i in chunks]
    kslc = [slice(h * dk, (h + 1) * dk) for h in range(heads)]
    vslc = [slice(h * dv, (h + 1) * dv) for h in range(heads)]

    for i in chunks:
        gc = g_s[rows[i], :]
        g1 = gc.astype(BF16)
        rem = gc - g1.astype(F32)
        g2 = rem.astype(BF16)
        g3 = (rem - g2.astype(F32)).astype(BF16)
        g_s[rows[i], :] = _dot(tri, g1) + _dot(tri, g2) + _dot(tri, g3)

    for i in chunks:
        b = g_s[rows[i], :]
        b_last = b[L - 1:L, :]
        qc = q_s[rows[i], :]
        kc = k_s[rows[i], :]
        qg_s[rows[i], :] = (qc * jnp.exp(b)).astype(BF16)
        kg_s[rows[i], :] = (kc * jnp.exp(-b)).astype(BF16)
        kdt_s[i, :, 0:L] = jnp.transpose(kc * jnp.exp(b_last - b)).astype(BF16)
        el_s[i] = jnp.broadcast_to(jnp.exp(b_last), el_s.shape[1:])

    for i in chunks:
        for h in range(heads):
            a = lax.dot_general(qg_s[rows[i], kslc[h]], kg_s[rows[i], kslc[h]],
                                (((1,), (1,)), ((), ())), preferred_element_type=F32)
            a_s[i * heads + h, 0:L, 0:L] = jnp.where(causal, a, 0.0).astype(BF16)

    for i in chunks:
        for h in range(heads):
            lhs = jnp.concatenate([a_s[i * heads + h, 0:L, 0:L], kdt_s[i, kslc[h], 0:L]], axis=0)
            res = _dot(lhs, v_s[rows[i], vslc[h]])
            o_s[rows[i], vslc[h]] = res[:L]
            u_s[i * heads + h] = res[L:]

    for i in chunks:
        bi = i // nc
        for h in range(heads):
            s_h = s_ref[bi, h]
            o_s[rows[i], vslc[h]] += _dot(qg_s[rows[i], kslc[h]], s_h.astype(BF16))
            decay = jnp.transpose(jnp.broadcast_to(el_s[i, 0:1, kslc[h]], (dk, dk)))
            s_ref[bi, h] = s_h * jnp.tile(decay, (1, dv // dk)) + u_s[i * heads + h]

    gate = _silu(_dot(xn_s[blk, :], win_ref[:, 2 * key + val:2 * key + 2 * val]))
    gn = gn_ref[...]
    o_n = jnp.concatenate([_rms(o_s[blk, vslc[h]], gn) for h in range(heads)], axis=1)
    y = _dot((o_n * gate).astype(BF16), wout_ref[...])
    y_ref[...] = x_ref[...] + _rms(y, post_ref[...])


def _gla_kernel(xp_ref, xs_ref, s0_ref, pre_ref, post_ref, win_ref, wg2_ref, bg_ref, gn_ref,
                wout_ref, yp_ref, ys_ref, sp_ref, ss_ref, *scratch,
                n_prompt, tiles_per_seq, chunk_p, batch_s, chunk_s):
    weights = (pre_ref, post_ref, win_ref, wg2_ref, bg_ref, gn_ref, wout_ref)

    def prompt():
        @pl.when(pl.program_id(0) % tiles_per_seq == 0)
        def _():
            sp_ref[...] = jnp.zeros(sp_ref.shape, F32)

        _gla_tile(xp_ref, yp_ref, sp_ref, *weights, *scratch,
                  nb=1, nc=xp_ref.shape[0] // chunk_p, L=chunk_p)

    def sample():
        ss_ref[...] = s0_ref[...]
        _gla_tile(xs_ref, ys_ref, ss_ref, *weights, *scratch,
                  nb=batch_s, nc=xs_ref.shape[0] // (batch_s * chunk_s), L=chunk_s)

    _two_streams(prompt, sample, n_prompt)


def _gla(xp, xs, s0, pre, post, w_in, w_gate2, b_gate, g_norm, w_out, *, batch_p, batch_s):
    mp, d = xp.shape
    ms = xs.shape[0]
    heads = GLA_HEADS
    key = w_gate2.shape[1]
    val = w_out.shape[0]
    seq_p, seq_s = mp // batch_p, ms // batch_s
    chunk_p, chunk_s = min(seq_p, GLA_CHUNK), min(seq_s, GLA_CHUNK)
    tile = TOKEN_TILE
    assert seq_p % tile == 0 and tile % chunk_p == 0 and seq_s % chunk_s == 0 and ms <= tile
    tiles_per_seq = seq_p // tile
    n_prompt = mp // tile
    n_chunks = max(tile // chunk_p, ms // chunk_s)
    chunk = max(chunk_p, chunk_s)
    state_shape = (heads, key // heads, val // heads)
    rows_p = _prompt_rows(tile, d, n_prompt)
    state_p = pl.BlockSpec((1,) + state_shape,
                           lambda i: (jnp.minimum(i, n_prompt - 1) // tiles_per_seq, 0, 0, 0))
    state_s = pl.BlockSpec((batch_s,) + state_shape, lambda i: (0, 0, 0, 0))
    kern = functools.partial(_gla_kernel, n_prompt=n_prompt, tiles_per_seq=tiles_per_seq,
                             chunk_p=chunk_p, batch_s=batch_s, chunk_s=chunk_s)
    return pl.pallas_call(
        kern,
        grid=(n_prompt + 1,),
        in_specs=[rows_p, _resident((ms, d)), _resident((batch_s,) + state_shape),
                  _resident((1, d)), _resident((1, d)), _resident(w_in.shape),
                  _resident(w_gate2.shape), _resident((1, key)), _resident((1, g_norm.shape[0])),
                  _resident(w_out.shape)],
        out_specs=[rows_p, pl.BlockSpec((ms, d), lambda i: (0, 0)), state_p, state_s],
        out_shape=[jax.ShapeDtypeStruct((mp, d), F32), jax.ShapeDtypeStruct((ms, d), F32),
                   jax.ShapeDtypeStruct((batch_p,) + state_shape, F32),
                   jax.ShapeDtypeStruct((batch_s,) + state_shape, F32)],
        scratch_shapes=[pltpu.VMEM((tile, key), F32), pltpu.VMEM((tile, key), F32),
                        pltpu.VMEM((tile, val), BF16), pltpu.VMEM((tile, key), F32),
                        pltpu.VMEM((tile, val), F32), pltpu.VMEM((tile, d), BF16),
                        pltpu.VMEM((tile, key), BF16), pltpu.VMEM((tile, key), BF16),
                        pltpu.VMEM((n_chunks, key, chunk), BF16),
                        pltpu.VMEM((n_chunks, 8, key), F32),
                        pltpu.VMEM((n_chunks * heads, chunk, chunk), BF16),
                        pltpu.VMEM((n_chunks * heads,) + state_shape[1:], F32)],
        compiler_params=_PARAMS,
        name="gla",
    )(xp, xs, s0, pre.reshape(1, d), post.reshape(1, d), w_in, w_gate2, b_gate.reshape(1, key),
      g_norm.reshape(1, -1), w_out)


def _sgu_tile(x_ref, y_ref, v_ref, pre_ref, post_ref, win_ref, bin_ref, lng_ref, lnb_ref,
              wmix_ref, bmix_ref, wout_ref, v_s, *, L):
    groups = SGU_GROUPS
    half = wout_ref.shape[0]
    gd = half // groups
    r = x_ref.shape[0]
    blk = slice(0, r)
    win = wmix_ref.shape[1]

    x = x_ref[...]
    xn = _rms(x, pre_ref[...]).astype(BF16)

    s1 = jnp.zeros((r, 1), F32)
    for g in range(groups):
        seg = slice(half + g * gd, half + (g + 1) * gd)
        vg = _gelu(_dot(xn, win_ref[:, seg]) + bin_ref[:, seg])
        v_s[blk, g * gd:(g + 1) * gd] = vg
        s1 = s1 + jnp.sum(vg, axis=-1, keepdims=True)
    mu = s1 / half
    s2 = jnp.zeros((r, 1), F32)
    for g in range(groups):
        xc = v_s[blk, g * gd:(g + 1) * gd] - mu
        s2 = s2 + jnp.sum(xc * xc, axis=-1, keepdims=True)
    rstd = lax.rsqrt(s2 / half + LN_EPS)

    row = lax.broadcasted_iota(jnp.int32, (win, win), 0)
    col = lax.broadcasted_iota(jnp.int32, (win, win), 1)
    mask = row >= col
    if L < win:
        mask = jnp.logical_and(mask, row // L == col // L)

    acc = jnp.zeros(x.shape, F32)
    for g in range(groups):
        seg = slice(g * gd, (g + 1) * gd)
        vn = (v_s[blk, seg] - mu) * rstd * lng_ref[:, seg] + lnb_ref[:, seg]
        if v_ref is not None:
            v_ref[:, seg] = vn
        vnb = vn.astype(BF16)
        wm = jnp.where(mask, wmix_ref[g], 0.0).astype(BF16)
        bcol = bmix_ref[:, g:g + 1]
        mixed = jnp.concatenate(
            [_dot(wm, vnb[c * win:(c + 1) * win, :]) + bcol for c in range(r // win)], axis=0)
        u = _gelu(_dot(xn, win_ref[:, seg]) + bin_ref[:, seg])
        acc = acc + _dot((u * mixed).astype(BF16), wout_ref[seg, :])
    y_ref[...] = x + _rms(acc, post_ref[...])


def _sgu_kernel(xp_ref, xs_ref, pre_ref, post_ref, win_ref, bin_ref, lng_ref, lnb_ref,
                wmix_p_ref, bmix_p_ref, wmix_s_ref, bmix_s_ref, wout_ref,
                yp_ref, ys_ref, vs_ref, v_s, *, n_prompt, chunk_p, chunk_s):
    shared = (pre_ref, post_ref, win_ref, bin_ref, lng_ref, lnb_ref)
    _two_streams(
        functools.partial(_sgu_tile, xp_ref, yp_ref, None, *shared, wmix_p_ref, bmix_p_ref,
                          wout_ref, v_s, L=chunk_p),
        functools.partial(_sgu_tile, xs_ref, ys_ref, vs_ref, *shared, wmix_s_ref, bmix_s_ref,
                          wout_ref, v_s, L=chunk_s),
        n_prompt)


def _sgu(xp, xs, pre, post, w_in, b_in, ln_g, ln_b, w_s, b_s, w_out, *, seq_p, seq_s):
    mp, d = xp.shape
    ms = xs.shape[0]
    half = w_out.shape[0]
    win = w_s.shape[1]
    chunk_p, chunk_s = min(seq_p, win), min(seq_s, win)
    tile = TOKEN_TILE
    assert chunk_p == win and tile % win == 0 and seq_p % win == 0
    assert ms % win == 0 and win % chunk_s == 0 and seq_s == chunk_s and ms <= tile
    n_prompt = mp // tile
    reps = win // chunk_s
    wmix_s = jnp.tile(w_s[:, :chunk_s, :chunk_s], (1, reps, reps))
    bmix_s = jnp.tile(b_s[:, :chunk_s].T, (reps, 1))
    rows_p = _prompt_rows(tile, d, n_prompt)
    kern = functools.partial(_sgu_kernel, n_prompt=n_prompt, chunk_p=chunk_p, chunk_s=chunk_s)
    return pl.pallas_call(
        kern,
        grid=(n_prompt + 1,),
        in_specs=[rows_p, _resident((ms, d)), _resident((1, d)), _resident((1, d)),
                  _resident(w_in.shape), _resident((1, 2 * half)), _resident((1, half)),
                  _resident((1, half)), _resident(w_s.shape), _resident((win, w_s.shape[0])),
                  _resident(w_s.shape), _resident((win, w_s.shape[0])), _resident(w_out.shape)],
        out_specs=[rows_p, pl.BlockSpec((ms, d), lambda i: (0, 0)),
                   pl.BlockSpec((ms, half), lambda i: (0, 0))],
        out_shape=[jax.ShapeDtypeStruct((mp, d), F32), jax.ShapeDtypeStruct((ms, d), F32),
                   jax.ShapeDtypeStruct((ms, half), F32)],
        scratch_shapes=[pltpu.VMEM((tile, half), F32)],
        compiler_params=_PARAMS,
        name="sgu",
    )(xp, xs, pre.reshape(1, d), post.reshape(1, d), w_in, b_in.reshape(1, -1),
      ln_g.reshape(1, -1), ln_b.reshape(1, -1), w_s, b_s.T, wmix_s, bmix_s, w_out)


def kernel(x_prompt, x_sample, state_gla, norm_pre, norm_post, ffn_w_gate, ffn_w_up, ffn_w_down,
           gla_w_in, gla_w_gate2, gla_b_gate, gla_norm, gla_w_out, sgu_w_in, sgu_b_in, sgu_ln_g,
           sgu_ln_b, sgu_w_s, sgu_b_s, sgu_w_out):
    batch, seq, d = x_prompt.shape
    dbatch, dseq, _ = x_sample.shape
    depth = norm_pre.shape[0]
    xp = x_prompt.reshape(batch * seq, d)
    xs = x_sample.reshape(dbatch * dseq, d)
    gla_p, gla_s, sgu_s = [], [], []

    def ffn(xp, xs, i, k, n):
        return _ffn(xp, xs, norm_pre[i, n], norm_post[i, n], ffn_w_gate[i, k].astype(BF16),
                    ffn_w_up[i, k].astype(BF16), ffn_w_down[i, k].astype(BF16))

    for i in range(depth):
        j = i // 2
        xp, xs = ffn(xp, xs, i, 0, 0)
        if i % 2 == 0:
            xp, xs, sp, ss = _gla(
                xp, xs, state_gla[j], norm_pre[i, 1], norm_post[i, 1], gla_w_in[j].astype(BF16),
                gla_w_gate2[j].astype(BF16), gla_b_gate[j], gla_norm[j],
                gla_w_out[j].astype(BF16), batch_p=batch, batch_s=dbatch)
            gla_p.append(sp)
            gla_s.append(ss)
        else:
            xp, xs, vs = _sgu(
                xp, xs, norm_pre[i, 1], norm_post[i, 1], sgu_w_in[j].astype(BF16), sgu_b_in[j],
                sgu_ln_g[j], sgu_ln_b[j], sgu_w_s[j], sgu_b_s[j], sgu_w_out[j].astype(BF16),
                seq_p=seq, seq_s=dseq)
            sgu_s.append(vs.reshape(dbatch, dseq, -1))
        xp, xs = ffn(xp, xs, i, 1, 2)

    return (xp.reshape(batch, seq, d), xs.reshape(dbatch, dseq, d),
            jnp.stack(gla_p), jnp.stack(gla_s), jnp.stack(sgu_s))
```

```python
import functools

import jax
import jax.numpy as jnp
from jax import lax
from jax.experimental import pallas as pl
from jax.experimental.pallas import tpu as pltpu

F32 = jnp.float32
BF16 = jnp.bfloat16

NORM_EPS = 1e-6
LN_EPS = 1e-5
GLA_HEADS = 4
GLA_TAU = 16.0
GLA_CHUNK = 64
SGU_CHUNK = 128
SGU_GROUPS = 4

VMEM_LIMIT_BYTES = 56 * 1024 * 1024
TOKEN_TILE = 512
FFN_COL_CHUNK = 512


def _dot(a, b):
    return jnp.dot(a, b, preferred_element_type=F32)


def _rms(x, g, eps=NORM_EPS):
    return x * lax.rsqrt(jnp.mean(x * x, axis=-1, keepdims=True) + eps) * g


def _silu(x):
    return x * jax.nn.sigmoid(x)


def _gelu(x):
    return 0.5 * x * (1.0 + lax.erf(x * (0.5 ** 0.5)))


def _resident(shape):
    zeros = (0,) * len(shape)
    return pl.BlockSpec(shape, lambda *_: zeros, pipeline_mode=pl.Buffered(1))


def _prompt_rows(tile, width, n_prompt):
    return pl.BlockSpec((tile, width), lambda i: (jnp.minimum(i, n_prompt - 1), 0))


def _two_streams(body_prompt, body_sample, n_prompt):
    step = pl.program_id(0)
    pl.when(step < n_prompt)(body_prompt)
    pl.when(step == n_prompt)(body_sample)


def _hbm():
    return pl.BlockSpec(memory_space=pl.ANY)


class _CastStream:
    def __init__(self, pieces, stage, sems):
        self.pieces, self.stage, self.sems = pieces, stage, sems

    def __len__(self):
        return len(self.pieces)

    def _copy(self, j):
        slot = j % 2
        return pltpu.make_async_copy(self.pieces[j][0], self.stage.at[slot], self.sems.at[slot])

    def start(self, j):
        if j < len(self.pieces):
            self._copy(j).start()

    def land(self, j):
        self._copy(j).wait()
        _, dst, idx = self.pieces[j]
        dst[idx] = self.stage[j % 2].astype(BF16)
        self.start(j + 2)


def _prime(streams):
    for s in streams:
        s.start(0)
        s.start(1)


def _land(streams, lo, hi):
    for j in range(lo, hi):
        for s in streams:
            if j < len(s):
                s.land(j)


def _col_pieces(hbm, lead, dst, unit):
    n = dst.shape[1] // unit
    return [(hbm.at[lead + (slice(None), pl.ds(j * unit, unit))], dst,
             (slice(None), slice(j * unit, (j + 1) * unit))) for j in range(n)]


def _row_pieces(hbm, lead, dst, unit):
    n = dst.shape[0] // unit
    return [(hbm.at[lead + (pl.ds(j * unit, unit), slice(None))], dst,
             (slice(j * unit, (j + 1) * unit), slice(None))) for j in range(n)]


def _stage(shape):
    return [pltpu.VMEM((2,) + shape, F32), pltpu.SemaphoreType.DMA((2,))]


_PARAMS = pltpu.CompilerParams(dimension_semantics=("arbitrary",),
                               vmem_limit_bytes=VMEM_LIMIT_BYTES)


def _ffn_tile(x_ref, o_ref, pre_ref, post_ref, wg_ref, wu_ref, wd_ref, before_chunk=None):
    x = x_ref[...]
    xn = _rms(x, pre_ref[...]).astype(BF16)
    d_ff = wg_ref.shape[1]
    acc = jnp.zeros(x.shape, F32)
    for c in range(d_ff // FFN_COL_CHUNK):
        if before_chunk is not None:
            before_chunk(c)
        cols = slice(c * FFN_COL_CHUNK, (c + 1) * FFN_COL_CHUNK)
        h = _silu(_dot(xn, wg_ref[:, cols])) * _dot(xn, wu_ref[:, cols])
        acc = acc + _dot(h.astype(BF16), wd_ref[cols, :])
    o_ref[...] = x + 0.5 * _rms(acc, post_ref[...])


def _ffn_kernel(xp_ref, xs_ref, pre_ref, post_ref, wg_hbm, wu_hbm, wd_hbm, op_ref, os_ref,
                wg_ref, wu_ref, wd_ref, stage_g, sem_g, stage_u, sem_u, stage_d, sem_d,
                *, n_prompt, lead):
    unit = stage_g.shape[2]
    per_chunk = FFN_COL_CHUNK // unit
    streams = [_CastStream(_col_pieces(wg_hbm, lead, wg_ref, unit), stage_g, sem_g),
               _CastStream(_col_pieces(wu_hbm, lead, wu_ref, unit), stage_u, sem_u),
               _CastStream(_row_pieces(wd_hbm, lead, wd_ref, unit), stage_d, sem_d)]
    weights = (pre_ref, post_ref, wg_ref, wu_ref, wd_ref)

    def first():
        _prime(streams)
        _ffn_tile(xp_ref, op_ref, *weights,
                  before_chunk=lambda c: _land(streams, c * per_chunk, (c + 1) * per_chunk))

    step = pl.program_id(0)
    pl.when(step == 0)(first)
    pl.when(jnp.logical_and(step > 0, step < n_prompt))(
        functools.partial(_ffn_tile, xp_ref, op_ref, *weights))
    pl.when(step == n_prompt)(functools.partial(_ffn_tile, xs_ref, os_ref, *weights))


def _ffn(xp, xs, pre, post, wg, wu, wd, lead):
    mp, d = xp.shape
    ms = xs.shape[0]
    d_ff = wg.shape[-1]
    n_prompt = mp // TOKEN_TILE
    unit = FFN_COL_CHUNK // 2
    rows_p = _prompt_rows(TOKEN_TILE, d, n_prompt)
    rows_s = _resident((ms, d))
    return pl.pallas_call(
        functools.partial(_ffn_kernel, n_prompt=n_prompt, lead=lead),
        grid=(n_prompt + 1,),
        in_specs=[rows_p, rows_s, _resident((1, d)), _resident((1, d)), _hbm(), _hbm(), _hbm()],
        out_specs=[rows_p, pl.BlockSpec((ms, d), lambda i: (0, 0))],
        out_shape=[jax.ShapeDtypeStruct((mp, d), F32), jax.ShapeDtypeStruct((ms, d), F32)],
        scratch_shapes=[pltpu.VMEM((d, d_ff), BF16), pltpu.VMEM((d, d_ff), BF16),
                        pltpu.VMEM((d_ff, d), BF16), *_stage((d, unit)), *_stage((d, unit)),
                        *_stage((unit, d))],
        compiler_params=_PARAMS,
        name="ffn",
    )(xp, xs, pre.reshape(1, d), post.reshape(1, d), wg, wu, wd)


def _gla_tile(x_ref, y_ref, s_ref, pre_ref, post_ref, win_ref, wg2_ref, bg_ref, gn_ref, wout_ref,
              q_s, k_s, v_s, g_s, o_s, xn_s, qg_s, kg_s, kdt_s, el_s, a_s, u_s, *, nb, nc, L):
    heads = GLA_HEADS
    key = wg2_ref.shape[1]
    val = wout_ref.shape[0]
    dk, dv = key // heads, val // heads
    r = nb * nc * L
    blk = slice(0, r)

    xn = _rms(x_ref[...], pre_ref[...]).astype(BF16)
    xn_s[blk, :] = xn
    q_s[blk, :] = _dot(xn, win_ref[:, 0:key]) * (dk ** -0.5)
    k_s[blk, :] = _dot(xn, win_ref[:, key:2 * key])
    v_s[blk, :] = _dot(xn, win_ref[:, 2 * key:2 * key + val]).astype(BF16)
    zg = _dot(xn, win_ref[:, 2 * key + 2 * val:])
    z = _dot(zg.astype(BF16), wg2_ref[...].astype(BF16)) + bg_ref[...]
    g_s[blk, :] = jax.nn.log_sigmoid(z) / GLA_TAU

    row = lax.broadcasted_iota(jnp.int32, (L, L), 0)
    col = lax.broadcasted_iota(jnp.int32, (L, L), 1)
    causal = row >= col
    tri = jnp.where(causal, 1.0, 0.0).astype(BF16)

    chunks = range(nb * nc)
    rows = [slice(i * L, (i + 1) * L) for i in chunks]
    kslc = [slice(h * dk, (h + 1) * dk) for h in range(heads)]
    vslc = [slice(h * dv, (h + 1) * dv) for h in range(heads)]

    for i in chunks:
        gc = g_s[rows[i], :]
        g1 = gc.astype(BF16)
        rem = gc - g1.astype(F32)
        g2 = rem.astype(BF16)
        g3 = (rem - g2.astype(F32)).astype(BF16)
        g_s[rows[i], :] = _dot(tri, g1) + _dot(tri, g2) + _dot(tri, g3)

    for i in chunks:
        b = g_s[rows[i], :]
        b_last = b[L - 1:L, :]
        qc = q_s[rows[i], :]
        kc = k_s[rows[i], :]
        qg_s[rows[i], :] = (qc * jnp.exp(b)).astype(BF16)
        kg_s[rows[i], :] = (kc * jnp.exp(-b)).astype(BF16)
        kdt_s[i, :, 0:L] = jnp.transpose(kc * jnp.exp(b_last - b)).astype(BF16)
        el_s[i] = jnp.broadcast_to(jnp.exp(b_last), el_s.shape[1:])

    for i in chunks:
        for h in range(heads):
            a = lax.dot_general(qg_s[rows[i], kslc[h]], kg_s[rows[i], kslc[h]],
                                (((1,), (1,)), ((), ())), preferred_element_type=F32)
            a_s[i * heads + h, 0:L, 0:L] = jnp.where(causal, a, 0.0).astype(BF16)

    for i in chunks:
        for h in range(heads):
            lhs = jnp.concatenate([a_s[i * heads + h, 0:L, 0:L], kdt_s[i, kslc[h], 0:L]], axis=0)
            res = _dot(lhs, v_s[rows[i], vslc[h]])
            o_s[rows[i], vslc[h]] = res[:L]
            u_s[i * heads + h] = res[L:]

    for i in chunks:
        bi = i // nc
        for h in range(heads):
            s_h = s_ref[bi, h]
            o_s[rows[i], vslc[h]] += _dot(qg_s[rows[i], kslc[h]], s_h.astype(BF16))
            decay = jnp.transpose(jnp.broadcast_to(el_s[i, 0:1, kslc[h]], (dk, dk)))
            s_ref[bi, h] = s_h * jnp.tile(decay, (1, dv // dk)) + u_s[i * heads + h]

    gate = _silu(_dot(xn_s[blk, :], win_ref[:, 2 * key + val:2 * key + 2 * val]))
    gn = gn_ref[...]
    o_n = jnp.concatenate([_rms(o_s[blk, vslc[h]], gn) for h in range(heads)], axis=1)
    y = _dot((o_n * gate).astype(BF16), wout_ref[...])
    y_ref[...] = x_ref[...] + _rms(y, post_ref[...])


def _gla_kernel(xp_ref, xs_ref, s0_ref, pre_ref, post_ref, win_hbm, wg2_ref, bg_ref, gn_ref,
                wout_hbm, yp_ref, ys_ref, sp_ref, ss_ref,
                win_ref, wout_ref, stage_i, sem_i, stage_o, sem_o, *scratch,
                n_prompt, tiles_per_seq, chunk_p, batch_s, chunk_s, lead):
    weights = (pre_ref, post_ref, win_ref, wg2_ref, bg_ref, gn_ref, wout_ref)

    @pl.when(pl.program_id(0) == 0)
    def _():
        streams = [
            _CastStream(_row_pieces(win_hbm, lead, win_ref, stage_i.shape[1]), stage_i, sem_i),
            _CastStream(_row_pieces(wout_hbm, lead, wout_ref, stage_o.shape[1]), stage_o, sem_o)]
        _prime(streams)
        _land(streams, 0, max(len(s) for s in streams))

    def prompt():
        @pl.when(pl.program_id(0) % tiles_per_seq == 0)
        def _():
            sp_ref[...] = jnp.zeros(sp_ref.shape, F32)

        _gla_tile(xp_ref, yp_ref, sp_ref, *weights, *scratch,
                  nb=1, nc=xp_ref.shape[0] // chunk_p, L=chunk_p)

    def sample():
        ss_ref[...] = s0_ref[...]
        _gla_tile(xs_ref, ys_ref, ss_ref, *weights, *scratch,
                  nb=batch_s, nc=xs_ref.shape[0] // (batch_s * chunk_s), L=chunk_s)

    _two_streams(prompt, sample, n_prompt)


def _gla(xp, xs, s0, pre, post, w_in, w_gate2, b_gate, g_norm, w_out, lead, *, batch_p, batch_s):
    mp, d = xp.shape
    ms = xs.shape[0]
    heads = GLA_HEADS
    key = w_gate2.shape[1]
    val = w_out.shape[-2]
    unit = 128
    seq_p, seq_s = mp // batch_p, ms // batch_s
    chunk_p, chunk_s = min(seq_p, GLA_CHUNK), min(seq_s, GLA_CHUNK)
    tile = TOKEN_TILE
    assert seq_p % tile == 0 and tile % chunk_p == 0 and seq_s % chunk_s == 0 and ms <= tile
    tiles_per_seq = seq_p // tile
    n_prompt = mp // tile
    n_chunks = max(tile // chunk_p, ms // chunk_s)
    chunk = max(chunk_p, chunk_s)
    state_shape = (heads, key // heads, val // heads)
    rows_p = _prompt_rows(tile, d, n_prompt)
    state_p = pl.BlockSpec((1,) + state_shape,
                           lambda i: (jnp.minimum(i, n_prompt - 1) // tiles_per_seq, 0, 0, 0))
    state_s = pl.BlockSpec((batch_s,) + state_shape, lambda i: (0, 0, 0, 0))
    kern = functools.partial(_gla_kernel, n_prompt=n_prompt, tiles_per_seq=tiles_per_seq,
                             chunk_p=chunk_p, batch_s=batch_s, chunk_s=chunk_s, lead=lead)
    return pl.pallas_call(
        kern,
        grid=(n_prompt + 1,),
        in_specs=[rows_p, _resident((ms, d)), _resident((batch_s,) + state_shape),
                  _resident((1, d)), _resident((1, d)), _hbm(),
                  _resident(w_gate2.shape), _resident((1, key)), _resident((1, g_norm.shape[0])),
                  _hbm()],
        out_specs=[rows_p, pl.BlockSpec((ms, d), lambda i: (0, 0)), state_p, state_s],
        out_shape=[jax.ShapeDtypeStruct((mp, d), F32), jax.ShapeDtypeStruct((ms, d), F32),
                   jax.ShapeDtypeStruct((batch_p,) + state_shape, F32),
                   jax.ShapeDtypeStruct((batch_s,) + state_shape, F32)],
        scratch_shapes=[pltpu.VMEM(w_in.shape[-2:], BF16), pltpu.VMEM(w_out.shape[-2:], BF16),
                        *_stage((unit, w_in.shape[-1])), *_stage((unit, d)),
                        pltpu.VMEM((tile, key), F32), pltpu.VMEM((tile, key), F32),
                        pltpu.VMEM((tile, val), BF16), pltpu.VMEM((tile, key), F32),
                        pltpu.VMEM((tile, val), F32), pltpu.VMEM((tile, d), BF16),
                        pltpu.VMEM((tile, key), BF16), pltpu.VMEM((tile, key), BF16),
                        pltpu.VMEM((n_chunks, key, chunk), BF16),
                        pltpu.VMEM((n_chunks, 8, key), F32),
                        pltpu.VMEM((n_chunks * heads, chunk, chunk), BF16),
                        pltpu.VMEM((n_chunks * heads,) + state_shape[1:], F32)],
        compiler_params=_PARAMS,
        name="gla",
    )(xp, xs, s0, pre.reshape(1, d), post.reshape(1, d), w_in, w_gate2, b_gate.reshape(1, key),
      g_norm.reshape(1, -1), w_out)


def _sgu_tile(x_ref, y_ref, v_ref, pre_ref, post_ref, win_ref, bin_ref, lng_ref, lnb_ref,
              wmix_ref, bmix_ref, wout_ref, v_s, *, L):
    groups = SGU_GROUPS
    half = wout_ref.shape[0]
    gd = half // groups
    r = x_ref.shape[0]
    blk = slice(0, r)
    win = wmix_ref.shape[1]

    x = x_ref[...]
    xn = _rms(x, pre_ref[...]).astype(BF16)

    s1 = jnp.zeros((r, 1), F32)
    for g in range(groups):
        seg = slice(half + g * gd, half + (g + 1) * gd)
        vg = _gelu(_dot(xn, win_ref[:, seg]) + bin_ref[:, seg])
        v_s[blk, g * gd:(g + 1) * gd] = vg
        s1 = s1 + jnp.sum(vg, axis=-1, keepdims=True)
    mu = s1 / half
    s2 = jnp.zeros((r, 1), F32)
    for g in range(groups):
        xc = v_s[blk, g * gd:(g + 1) * gd] - mu
        s2 = s2 + jnp.sum(xc * xc, axis=-1, keepdims=True)
    rstd = lax.rsqrt(s2 / half + LN_EPS)

    row = lax.broadcasted_iota(jnp.int32, (win, win), 0)
    col = lax.broadcasted_iota(jnp.int32, (win, win), 1)
    mask = row >= col
    if L < win:
        mask = jnp.logical_and(mask, row // L == col // L)

    acc = jnp.zeros(x.shape, F32)
    for g in range(groups):
        seg = slice(g * gd, (g + 1) * gd)
        vn = (v_s[blk, seg] - mu) * rstd * lng_ref[:, seg] + lnb_ref[:, seg]
        if v_ref is not None:
            v_ref[:, seg] = vn
        vnb = vn.astype(BF16)
        wm = jnp.where(mask, wmix_ref[g], 0.0).astype(BF16)
        bcol = bmix_ref[:, g:g + 1]
        mixed = jnp.concatenate(
            [_dot(wm, vnb[c * win:(c + 1) * win, :]) + bcol for c in range(r // win)], axis=0)
        u = _gelu(_dot(xn, win_ref[:, seg]) + bin_ref[:, seg])
        acc = acc + _dot((u * mixed).astype(BF16), wout_ref[seg, :])
    y_ref[...] = x + _rms(acc, post_ref[...])


def _sgu_kernel(xp_ref, xs_ref, pre_ref, post_ref, win_hbm, bin_ref, lng_ref, lnb_ref,
                wmix_p_ref, bmix_p_ref, wmix_s_ref, bmix_s_ref, wout_hbm,
                yp_ref, ys_ref, vs_ref, win_ref, wout_ref, stage_i, sem_i, stage_o, sem_o, v_s,
                *, n_prompt, chunk_p, chunk_s, lead):
    shared = (pre_ref, post_ref, win_ref, bin_ref, lng_ref, lnb_ref)

    @pl.when(pl.program_id(0) == 0)
    def _():
        streams = [
            _CastStream(_col_pieces(win_hbm, lead, win_ref, stage_i.shape[2]), stage_i, sem_i),
            _CastStream(_row_pieces(wout_hbm, lead, wout_ref, stage_o.shape[1]), stage_o, sem_o)]
        _prime(streams)
        _land(streams, 0, max(len(s) for s in streams))

    _two_streams(
        functools.partial(_sgu_tile, xp_ref, yp_ref, None, *shared, wmix_p_ref, bmix_p_ref,
                          wout_ref, v_s, L=chunk_p),
        functools.partial(_sgu_tile, xs_ref, ys_ref, vs_ref, *shared, wmix_s_ref, bmix_s_ref,
                          wout_ref, v_s, L=chunk_s),
        n_prompt)


def _sgu(xp, xs, pre, post, w_in, b_in, ln_g, ln_b, w_s, b_s, w_out, lead, *, seq_p, seq_s):
    mp, d = xp.shape
    ms = xs.shape[0]
    half = w_out.shape[-2]
    unit = half // 8
    win = w_s.shape[1]
    chunk_p, chunk_s = min(seq_p, win), min(seq_s, win)
    tile = TOKEN_TILE
    assert chunk_p == win and tile % win == 0 and seq_p % win == 0
    assert ms % win == 0 and win % chunk_s == 0 and seq_s == chunk_s and ms <= tile
    n_prompt = mp // tile
    reps = win // chunk_s
    wmix_s = jnp.tile(w_s[:, :chunk_s, :chunk_s], (1, reps, reps))
    bmix_s = jnp.tile(b_s[:, :chunk_s].T, (reps, 1))
    rows_p = _prompt_rows(tile, d, n_prompt)
    kern = functools.partial(_sgu_kernel, n_prompt=n_prompt, chunk_p=chunk_p, chunk_s=chunk_s,
                             lead=lead)
    return pl.pallas_call(
        kern,
        grid=(n_prompt + 1,),
        in_specs=[rows_p, _resident((ms, d)), _resident((1, d)), _resident((1, d)),
                  _hbm(), _resident((1, 2 * half)), _resident((1, half)),
                  _resident((1, half)), _resident(w_s.shape), _resident((win, w_s.shape[0])),
                  _resident(w_s.shape), _resident((win, w_s.shape[0])), _hbm()],
        out_specs=[rows_p, pl.BlockSpec((ms, d), lambda i: (0, 0)),
                   pl.BlockSpec((ms, half), lambda i: (0, 0))],
        out_shape=[jax.ShapeDtypeStruct((mp, d), F32), jax.ShapeDtypeStruct((ms, d), F32),
                   jax.ShapeDtypeStruct((ms, half), F32)],
        scratch_shapes=[pltpu.VMEM((d, 2 * half), BF16), pltpu.VMEM((half, d), BF16),
                        *_stage((d, unit)), *_stage((unit, d)), pltpu.VMEM((tile, half), F32)],
        compiler_params=_PARAMS,
        name="sgu",
    )(xp, xs, pre.reshape(1, d), post.reshape(1, d), w_in, b_in.reshape(1, -1),
      ln_g.reshape(1, -1), ln_b.reshape(1, -1), w_s, b_s.T, wmix_s, bmix_s, w_out)


def kernel(x_prompt, x_sample, state_gla, norm_pre, norm_post, ffn_w_gate, ffn_w_up, ffn_w_down,
           gla_w_in, gla_w_gate2, gla_b_gate, gla_norm, gla_w_out, sgu_w_in, sgu_b_in, sgu_ln_g,
           sgu_ln_b, sgu_w_s, sgu_b_s, sgu_w_out):
    batch, seq, d = x_prompt.shape
    dbatch, dseq, _ = x_sample.shape
    depth = norm_pre.shape[0]
    xp = x_prompt.reshape(batch * seq, d)
    xs = x_sample.reshape(dbatch * dseq, d)
    gla_p, gla_s, sgu_s = [], [], []

    def ffn(xp, xs, i, k, n):
        return _ffn(xp, xs, norm_pre[i, n], norm_post[i, n], ffn_w_gate, ffn_w_up, ffn_w_down,
                    (i, k))

    for i in range(depth):
        j = i // 2
        xp, xs = ffn(xp, xs, i, 0, 0)
        if i % 2 == 0:
            xp, xs, sp, ss = _gla(
                xp, xs, state_gla[j], norm_pre[i, 1], norm_post[i, 1], gla_w_in, gla_w_gate2[j],
                gla_b_gate[j], gla_norm[j], gla_w_out, (j,), batch_p=batch, batch_s=dbatch)
            gla_p.append(sp)
            gla_s.append(ss)
        else:
            xp, xs, vs = _sgu(
                xp, xs, norm_pre[i, 1], norm_post[i, 1], sgu_w_in, sgu_b_in[j], sgu_ln_g[j],
                sgu_ln_b[j], sgu_w_s[j], sgu_b_s[j], sgu_w_out, (j,), seq_p=seq, seq_s=dseq)
            sgu_s.append(vs.reshape(dbatch, dseq, -1))
        xp, xs = ffn(xp, xs, i, 1, 2)

    return (xp.reshape(batch, seq, d), xs.reshape(dbatch, dseq, d),
            jnp.stack(gla_p), jnp.stack(gla_s), jnp.stack(sgu_s))
```

```python
import functools

import jax
import jax.numpy as jnp
from jax import lax
from jax.experimental import pallas as pl
from jax.experimental.pallas import tpu as pltpu

F32 = jnp.float32
BF16 = jnp.bfloat16

NORM_EPS = 1e-6
LN_EPS = 1e-5
GLA_HEADS = 4
GLA_TAU = 16.0
GLA_CHUNK = 64
SGU_CHUNK = 128
SGU_GROUPS = 4

VMEM_LIMIT_BYTES = 56 * 1024 * 1024
TOKEN_TILE = 512
FFN_COL_CHUNK = 512


def _dot(a, b):
    return jnp.dot(a, b, preferred_element_type=F32)


def _rms(x, g, eps=NORM_EPS):
    return x * lax.rsqrt(jnp.mean(x * x, axis=-1, keepdims=True) + eps) * g


def _silu(x):
    return x * jax.nn.sigmoid(x)


def _gelu(x):
    return 0.5 * x * (1.0 + lax.erf(x * (0.5 ** 0.5)))


def _resident(shape):
    zeros = (0,) * len(shape)
    return pl.BlockSpec(shape, lambda *_: zeros, pipeline_mode=pl.Buffered(1))


def _prompt_rows(tile, width, n_prompt):
    return pl.BlockSpec((tile, width), lambda i: (jnp.minimum(i, n_prompt - 1), 0))


def _two_streams(body_prompt, body_sample, n_prompt):
    step = pl.program_id(0)
    pl.when(step < n_prompt)(body_prompt)
    pl.when(step == n_prompt)(body_sample)


def _hbm():
    return pl.BlockSpec(memory_space=pl.ANY)


class _CastStream:
    def __init__(self, pieces, stage, sems):
        self.pieces, self.stage, self.sems = pieces, stage, sems

    def __len__(self):
        return len(self.pieces)

    def _copy(self, j):
        slot = j % 2
        return pltpu.make_async_copy(self.pieces[j][0], self.stage.at[slot], self.sems.at[slot])

    def start(self, j):
        if j < len(self.pieces):
            self._copy(j).start()

    def land(self, j):
        self._copy(j).wait()
        _, dst, idx = self.pieces[j]
        dst[idx] = self.stage[j % 2].astype(BF16)
        self.start(j + 2)


def _prime(streams):
    for s in streams:
        s.start(0)
        s.start(1)


def _land(streams, lo, hi):
    for j in range(lo, hi):
        for s in streams:
            if j < len(s):
                s.land(j)


def _row_pieces(hbm, lead, dst, unit):
    n = dst.shape[0] // unit
    return [(hbm.at[lead + (pl.ds(j * unit, unit), slice(None))], dst,
             (slice(j * unit, (j + 1) * unit), slice(None))) for j in range(n)]


def _stage(shape):
    return [pltpu.VMEM((2,) + shape, F32), pltpu.SemaphoreType.DMA((2,))]


_PARAMS = pltpu.CompilerParams(dimension_semantics=("arbitrary",),
                               vmem_limit_bytes=VMEM_LIMIT_BYTES)


def _ffn_tile(x_ref, o_ref, pre_ref, post_ref, wg_ref, wu_ref, wd_ref, before_chunk=None):
    x = x_ref[...]
    xn = _rms(x, pre_ref[...]).astype(BF16)
    d_ff = wg_ref.shape[1]
    acc = jnp.zeros(x.shape, F32)
    for c in range(d_ff // FFN_COL_CHUNK):
        if before_chunk is not None:
            before_chunk(c)
        cols = slice(c * FFN_COL_CHUNK, (c + 1) * FFN_COL_CHUNK)
        h = _silu(_dot(xn, wg_ref[:, cols])) * _dot(xn, wu_ref[:, cols])
        acc = acc + _dot(h.astype(BF16), wd_ref[cols, :])
    o_ref[...] = x + 0.5 * _rms(acc, post_ref[...])


def _ffn_kernel(xp_ref, xs_ref, pre_ref, post_ref, wg_hbm, wu_hbm, wd_hbm, op_ref, os_ref,
                wg_ref, wu_ref, wd_ref, stage_g, sem_g, stage_u, sem_u, stage_d, sem_d,
                *, n_prompt, lead):
    per_chunk = FFN_COL_CHUNK // stage_d.shape[1]
    gate_up = [_CastStream(_row_pieces(wg_hbm, lead, wg_ref, stage_g.shape[1]), stage_g, sem_g),
               _CastStream(_row_pieces(wu_hbm, lead, wu_ref, stage_u.shape[1]), stage_u, sem_u)]
    down = [_CastStream(_row_pieces(wd_hbm, lead, wd_ref, stage_d.shape[1]), stage_d, sem_d)]
    weights = (pre_ref, post_ref, wg_ref, wu_ref, wd_ref)

    def first():
        _prime(gate_up + down)
        _land(gate_up, 0, len(gate_up[0]))
        _ffn_tile(xp_ref, op_ref, *weights,
                  before_chunk=lambda c: _land(down, c * per_chunk, (c + 1) * per_chunk))

    step = pl.program_id(0)
    pl.when(step == 0)(first)
    pl.when(jnp.logical_and(step > 0, step < n_prompt))(
        functools.partial(_ffn_tile, xp_ref, op_ref, *weights))
    pl.when(step == n_prompt)(functools.partial(_ffn_tile, xs_ref, os_ref, *weights))


def _ffn(xp, xs, pre, post, wg, wu, wd, lead):
    mp, d = xp.shape
    ms = xs.shape[0]
    d_ff = wg.shape[-1]
    n_prompt = mp // TOKEN_TILE
    unit_d = FFN_COL_CHUNK // 2
    unit_gu = unit_d * d // d_ff
    rows_p = _prompt_rows(TOKEN_TILE, d, n_prompt)
    rows_s = _resident((ms, d))
    return pl.pallas_call(
        functools.partial(_ffn_kernel, n_prompt=n_prompt, lead=lead),
        grid=(n_prompt + 1,),
        in_specs=[rows_p, rows_s, _resident((1, d)), _resident((1, d)), _hbm(), _hbm(), _hbm()],
        out_specs=[rows_p, pl.BlockSpec((ms, d), lambda i: (0, 0))],
        out_shape=[jax.ShapeDtypeStruct((mp, d), F32), jax.ShapeDtypeStruct((ms, d), F32)],
        scratch_shapes=[pltpu.VMEM((d, d_ff), BF16), pltpu.VMEM((d, d_ff), BF16),
                        pltpu.VMEM((d_ff, d), BF16), *_stage((unit_gu, d_ff)),
                        *_stage((unit_gu, d_ff)), *_stage((unit_d, d))],
        compiler_params=_PARAMS,
        name="ffn",
    )(xp, xs, pre.reshape(1, d), post.reshape(1, d), wg, wu, wd)


def _gla_tile(x_ref, y_ref, s_ref, pre_ref, post_ref, win_ref, wg2_ref, bg_ref, gn_ref, wout_ref,
              q_s, k_s, v_s, g_s, o_s, xn_s, qg_s, kg_s, kdt_s, el_s, a_s, u_s, *, nb, nc, L):
    heads = GLA_HEADS
    key = wg2_ref.shape[1]
    val = wout_ref.shape[0]
    dk, dv = key // heads, val // heads
    r = nb * nc * L
    blk = slice(0, r)

    xn = _rms(x_ref[...], pre_ref[...]).astype(BF16)
    xn_s[blk, :] = xn
    zg = _dot(xn, win_ref[:, 2 * key + 2 * val:])
    z = _dot(zg.astype(BF16), wg2_ref[...].astype(BF16)) + bg_ref[...]
    g_s[blk, :] = jax.nn.log_sigmoid(z) / GLA_TAU
    q_s[blk, :] = _dot(xn, win_ref[:, 0:key]) * (dk ** -0.5)
    k_s[blk, :] = _dot(xn, win_ref[:, key:2 * key])

    row = lax.broadcasted_iota(jnp.int32, (L, L), 0)
    col = lax.broadcasted_iota(jnp.int32, (L, L), 1)
    causal = row >= col
    tri = jnp.where(causal, 1.0, 0.0).astype(BF16)

    chunks = range(nb * nc)
    rows = [slice(i * L, (i + 1) * L) for i in chunks]
    kslc = [slice(h * dk, (h + 1) * dk) for h in range(heads)]
    vslc = [slice(h * dv, (h + 1) * dv) for h in range(heads)]

    for i in chunks:
        gc = g_s[rows[i], :]
        g1 = gc.astype(BF16)
        rem = gc - g1.astype(F32)
        g2 = rem.astype(BF16)
        g3 = (rem - g2.astype(F32)).astype(BF16)
        g_s[rows[i], :] = _dot(tri, g1) + _dot(tri, g2) + _dot(tri, g3)

    v_s[blk, :] = _dot(xn_s[blk, :], win_ref[:, 2 * key:2 * key + val]).astype(BF16)

    for i in chunks:
        b = g_s[rows[i], :]
        b_last = b[L - 1:L, :]
        qc = q_s[rows[i], :]
        kc = k_s[rows[i], :]
        qg_s[rows[i], :] = (qc * jnp.exp(b)).astype(BF16)
        kg_s[rows[i], :] = (kc * jnp.exp(-b)).astype(BF16)
        kdt_s[i, :, 0:L] = jnp.transpose(kc * jnp.exp(b_last - b)).astype(BF16)
        el_s[i] = jnp.broadcast_to(jnp.exp(b_last), el_s.shape[1:])

    for i in chunks:
        for h in range(heads):
            a = lax.dot_general(qg_s[rows[i], kslc[h]], kg_s[rows[i], kslc[h]],
                                (((1,), (1,)), ((), ())), preferred_element_type=F32)
            a_s[i * heads + h, 0:L, 0:L] = jnp.where(causal, a, 0.0).astype(BF16)

    for i in chunks:
        for h in range(heads):
            lhs = jnp.concatenate([a_s[i * heads + h, 0:L, 0:L], kdt_s[i, kslc[h], 0:L]], axis=0)
            res = _dot(lhs, v_s[rows[i], vslc[h]])
            o_s[rows[i], vslc[h]] = res[:L]
            u_s[i * heads + h] = res[L:]

    gate = _silu(_dot(xn_s[blk, :], win_ref[:, 2 * key + val:2 * key + 2 * val]))

    for i in chunks:
        bi = i // nc
        for h in range(heads):
            s_h = s_ref[bi, h]
            o_s[rows[i], vslc[h]] += _dot(qg_s[rows[i], kslc[h]], s_h.astype(BF16))
            decay = jnp.transpose(jnp.broadcast_to(el_s[i, 0:1, kslc[h]], (dk, dk)))
            s_ref[bi, h] = s_h * jnp.tile(decay, (1, dv // dk)) + u_s[i * heads + h]

    gn = gn_ref[...]
    o_n = jnp.concatenate([_rms(o_s[blk, vslc[h]], gn) for h in range(heads)], axis=1)
    y = _dot((o_n * gate).astype(BF16), wout_ref[...])
    y_ref[...] = x_ref[...] + _rms(y, post_ref[...])


def _gla_kernel(xp_ref, xs_ref, s0_ref, pre_ref, post_ref, win_hbm, wg2_ref, bg_ref, gn_ref,
                wout_hbm, yp_ref, ys_ref, sp_ref, ss_ref,
                win_ref, wout_ref, stage_i, sem_i, stage_o, sem_o, *scratch,
                n_prompt, tiles_per_seq, chunk_p, batch_s, chunk_s, lead):
    weights = (pre_ref, post_ref, win_ref, wg2_ref, bg_ref, gn_ref, wout_ref)

    @pl.when(pl.program_id(0) == 0)
    def _():
        streams = [
            _CastStream(_row_pieces(win_hbm, lead, win_ref, stage_i.shape[1]), stage_i, sem_i),
            _CastStream(_row_pieces(wout_hbm, lead, wout_ref, stage_o.shape[1]), stage_o, sem_o)]
        _prime(streams)
        _land(streams, 0, max(len(s) for s in streams))

    def prompt():
        @pl.when(pl.program_id(0) % tiles_per_seq == 0)
        def _():
            sp_ref[...] = jnp.zeros(sp_ref.shape, F32)

        _gla_tile(xp_ref, yp_ref, sp_ref, *weights, *scratch,
                  nb=1, nc=xp_ref.shape[0] // chunk_p, L=chunk_p)

    def sample():
        ss_ref[...] = s0_ref[...]
        _gla_tile(xs_ref, ys_ref, ss_ref, *weights, *scratch,
                  nb=batch_s, nc=xs_ref.shape[0] // (batch_s * chunk_s), L=chunk_s)

    _two_streams(prompt, sample, n_prompt)


def _gla(xp, xs, s0, pre, post, w_in, w_gate2, b_gate, g_norm, w_out, lead, *, batch_p, batch_s):
    mp, d = xp.shape
    ms = xs.shape[0]
    heads = GLA_HEADS
    key = w_gate2.shape[1]
    val = w_out.shape[-2]
    unit = 128
    seq_p, seq_s = mp // batch_p, ms // batch_s
    chunk_p, chunk_s = min(seq_p, GLA_CHUNK), min(seq_s, GLA_CHUNK)
    tile = TOKEN_TILE
    assert seq_p % tile == 0 and tile % chunk_p == 0 and seq_s % chunk_s == 0 and ms <= tile
    tiles_per_seq = seq_p // tile
    n_prompt = mp // tile
    n_chunks = max(tile // chunk_p, ms // chunk_s)
    chunk = max(chunk_p, chunk_s)
    state_shape = (heads, key // heads, val // heads)
    rows_p = _prompt_rows(tile, d, n_prompt)
    state_p = pl.BlockSpec((1,) + state_shape,
                           lambda i: (jnp.minimum(i, n_prompt - 1) // tiles_per_seq, 0, 0, 0))
    state_s = pl.BlockSpec((batch_s,) + state_shape, lambda i: (0, 0, 0, 0))
    kern = functools.partial(_gla_kernel, n_prompt=n_prompt, tiles_per_seq=tiles_per_seq,
                             chunk_p=chunk_p, batch_s=batch_s, chunk_s=chunk_s, lead=lead)
    return pl.pallas_call(
        kern,
        grid=(n_prompt + 1,),
        in_specs=[rows_p, _resident((ms, d)), _resident((batch_s,) + state_shape),
                  _resident((1, d)), _resident((1, d)), _hbm(),
                  _resident(w_gate2.shape), _resident((1, key)), _resident((1, g_norm.shape[0])),
                  _hbm()],
        out_specs=[rows_p, pl.BlockSpec((ms, d), lambda i: (0, 0)), state_p, state_s],
        out_shape=[jax.ShapeDtypeStruct((mp, d), F32), jax.ShapeDtypeStruct((ms, d), F32),
                   jax.ShapeDtypeStruct((batch_p,) + state_shape, F32),
                   jax.ShapeDtypeStruct((batch_s,) + state_shape, F32)],
        scratch_shapes=[pltpu.VMEM(w_in.shape[-2:], BF16), pltpu.VMEM(w_out.shape[-2:], BF16),
                        *_stage((unit, w_in.shape[-1])), *_stage((unit, d)),
                        pltpu.VMEM((tile, key), F32), pltpu.VMEM((tile, key), F32),
                        pltpu.VMEM((tile, val), BF16), pltpu.VMEM((tile, key), F32),
                        pltpu.VMEM((tile, val), F32), pltpu.VMEM((tile, d), BF16),
                        pltpu.VMEM((tile, key), BF16), pltpu.VMEM((tile, key), BF16),
                        pltpu.VMEM((n_chunks, key, chunk), BF16),
                        pltpu.VMEM((n_chunks, 8, key), F32),
                        pltpu.VMEM((n_chunks * heads, chunk, chunk), BF16),
                        pltpu.VMEM((n_chunks * heads,) + state_shape[1:], F32)],
        compiler_params=_PARAMS,
        name="gla",
    )(xp, xs, s0, pre.reshape(1, d), post.reshape(1, d), w_in, w_gate2, b_gate.reshape(1, key),
      g_norm.reshape(1, -1), w_out)


def _sgu_tile(x_ref, y_ref, v_ref, pre_ref, post_ref, win_ref, bin_ref, lng_ref, lnb_ref,
              wmix_ref, bmix_ref, wout_ref, v_s, *, L):
    groups = SGU_GROUPS
    half = wout_ref.shape[0]
    gd = half // groups
    r = x_ref.shape[0]
    blk = slice(0, r)
    win = wmix_ref.shape[1]

    x = x_ref[...]
    xn = _rms(x, pre_ref[...]).astype(BF16)

    s1 = jnp.zeros((r, 1), F32)
    for g in range(groups):
        seg = slice(half + g * gd, half + (g + 1) * gd)
        vg = _gelu(_dot(xn, win_ref[:, seg]) + bin_ref[:, seg])
        v_s[blk, g * gd:(g + 1) * gd] = vg
        s1 = s1 + jnp.sum(vg, axis=-1, keepdims=True)
    mu = s1 / half

    def u_proj(g):
        seg = slice(g * gd, (g + 1) * gd)
        return _gelu(_dot(xn, win_ref[:, seg]) + bin_ref[:, seg])

    u_next = u_proj(0)
    s2 = jnp.zeros((r, 1), F32)
    for g in range(groups):
        xc = v_s[blk, g * gd:(g + 1) * gd] - mu
        s2 = s2 + jnp.sum(xc * xc, axis=-1, keepdims=True)
    rstd = lax.rsqrt(s2 / half + LN_EPS)

    row = lax.broadcasted_iota(jnp.int32, (win, win), 0)
    col = lax.broadcasted_iota(jnp.int32, (win, win), 1)
    mask = row >= col
    if L < win:
        mask = jnp.logical_and(mask, row // L == col // L)

    acc = jnp.zeros(x.shape, F32)
    for g in range(groups):
        seg = slice(g * gd, (g + 1) * gd)
        vn = (v_s[blk, seg] - mu) * rstd * lng_ref[:, seg] + lnb_ref[:, seg]
        if v_ref is not None:
            v_ref[:, seg] = vn
        vnb = vn.astype(BF16)
        wm = jnp.where(mask, wmix_ref[g], 0.0).astype(BF16)
        bcol = bmix_ref[:, g:g + 1]
        mixed = jnp.concatenate(
            [_dot(wm, vnb[c * win:(c + 1) * win, :]) + bcol for c in range(r // win)], axis=0)
        u = u_next
        if g + 1 < groups:
            u_next = u_proj(g + 1)
        acc = acc + _dot((u * mixed).astype(BF16), wout_ref[seg, :])
    y_ref[...] = x + _rms(acc, post_ref[...])


def _sgu_kernel(xp_ref, xs_ref, pre_ref, post_ref, win_hbm, bin_ref, lng_ref, lnb_ref,
                wmix_p_ref, bmix_p_ref, wmix_s_ref, bmix_s_ref, wout_hbm,
                yp_ref, ys_ref, vs_ref, win_ref, wout_ref, stage_i, sem_i, stage_o, sem_o, v_s,
                *, n_prompt, chunk_p, chunk_s, lead):
    shared = (pre_ref, post_ref, win_ref, bin_ref, lng_ref, lnb_ref)

    @pl.when(pl.program_id(0) == 0)
    def _():
        streams = [
            _CastStream(_row_pieces(win_hbm, lead, win_ref, stage_i.shape[1]), stage_i, sem_i),
            _CastStream(_row_pieces(wout_hbm, lead, wout_ref, stage_o.shape[1]), stage_o, sem_o)]
        _prime(streams)
        _land(streams, 0, max(len(s) for s in streams))

    _two_streams(
        functools.partial(_sgu_tile, xp_ref, yp_ref, None, *shared, wmix_p_ref, bmix_p_ref,
                          wout_ref, v_s, L=chunk_p),
        functools.partial(_sgu_tile, xs_ref, ys_ref, vs_ref, *shared, wmix_s_ref, bmix_s_ref,
                          wout_ref, v_s, L=chunk_s),
        n_prompt)


def _sgu(xp, xs, pre, post, w_in, b_in, ln_g, ln_b, w_s, b_s, w_out, lead, *, seq_p, seq_s):
    mp, d = xp.shape
    ms = xs.shape[0]
    half = w_out.shape[-2]
    unit = half // 8
    win = w_s.shape[1]
    chunk_p, chunk_s = min(seq_p, win), min(seq_s, win)
    tile = TOKEN_TILE
    assert chunk_p == win and tile % win == 0 and seq_p % win == 0
    assert ms % win == 0 and win % chunk_s == 0 and seq_s == chunk_s and ms <= tile
    n_prompt = mp // tile
    reps = win // chunk_s
    wmix_s = jnp.tile(w_s[:, :chunk_s, :chunk_s], (1, reps, reps))
    bmix_s = jnp.tile(b_s[:, :chunk_s].T, (reps, 1))
    rows_p = _prompt_rows(tile, d, n_prompt)
    kern = functools.partial(_sgu_kernel, n_prompt=n_prompt, chunk_p=chunk_p, chunk_s=chunk_s,
                             lead=lead)
    return pl.pallas_call(
        kern,
        grid=(n_prompt + 1,),
        in_specs=[rows_p, _resident((ms, d)), _resident((1, d)), _resident((1, d)),
                  _hbm(), _resident((1, 2 * half)), _resident((1, half)),
                  _resident((1, half)), _resident(w_s.shape), _resident((win, w_s.shape[0])),
                  _resident(w_s.shape), _resident((win, w_s.shape[0])), _hbm()],
        out_specs=[rows_p, pl.BlockSpec((ms, d), lambda i: (0, 0)),
                   pl.BlockSpec((ms, half), lambda i: (0, 0))],
        out_shape=[jax.ShapeDtypeStruct((mp, d), F32), jax.ShapeDtypeStruct((ms, d), F32),
                   jax.ShapeDtypeStruct((ms, half), F32)],
        scratch_shapes=[pltpu.VMEM((d, 2 * half), BF16), pltpu.VMEM((half, d), BF16),
                        *_stage((unit * d // (2 * half), 2 * half)), *_stage((unit, d)),
                        pltpu.VMEM((tile, half), F32)],
        compiler_params=_PARAMS,
        name="sgu",
    )(xp, xs, pre.reshape(1, d), post.reshape(1, d), w_in, b_in.reshape(1, -1),
      ln_g.reshape(1, -1), ln_b.reshape(1, -1), w_s, b_s.T, wmix_s, bmix_s, w_out)


def kernel(x_prompt, x_sample, state_gla, norm_pre, norm_post, ffn_w_gate, ffn_w_up, ffn_w_down,
           gla_w_in, gla_w_gate2, gla_b_gate, gla_norm, gla_w_out, sgu_w_in, sgu_b_in, sgu_ln_g,
           sgu_ln_b, sgu_w_s, sgu_b_s, sgu_w_out):
    batch, seq, d = x_prompt.shape
    dbatch, dseq, _ = x_sample.shape
    depth = norm_pre.shape[0]
    xp = x_prompt.reshape(batch * seq, d)
    xs = x_sample.reshape(dbatch * dseq, d)
    gla_p, gla_s, sgu_s = [], [], []

    def ffn(xp, xs, i, k, n):
        return _ffn(xp, xs, norm_pre[i, n], norm_post[i, n], ffn_w_gate, ffn_w_up, ffn_w_down,
                    (i, k))

    for i in range(depth):
        j = i // 2
        xp, xs = ffn(xp, xs, i, 0, 0)
        if i % 2 == 0:
            xp, xs, sp, ss = _gla(
                xp, xs, state_gla[j], norm_pre[i, 1], norm_post[i, 1], gla_w_in, gla_w_gate2[j],
                gla_b_gate[j], gla_norm[j], gla_w_out, (j,), batch_p=batch, batch_s=dbatch)
            gla_p.append(sp)
            gla_s.append(ss)
        else:
            xp, xs, vs = _sgu(
                xp, xs, norm_pre[i, 1], norm_post[i, 1], sgu_w_in, sgu_b_in[j], sgu_ln_g[j],
                sgu_ln_b[j], sgu_w_s[j], sgu_b_s[j], sgu_w_out, (j,), seq_p=seq, seq_s=dseq)
            sgu_s.append(vs.reshape(dbatch, dseq, -1))
        xp, xs = ffn(xp, xs, i, 1, 2)

    return (xp.reshape(batch, seq, d), xs.reshape(dbatch, dseq, d),
            jnp.stack(gla_p), jnp.stack(gla_s), jnp.stack(sgu_s))
```

```python
import functools

import jax
import jax.numpy as jnp
from jax import lax
from jax.experimental import pallas as pl
from jax.experimental.pallas import tpu as pltpu

F32 = jnp.float32
BF16 = jnp.bfloat16

NORM_EPS = 1e-6
LN_EPS = 1e-5
GLA_HEADS = 4
GLA_TAU = 16.0
GLA_CHUNK = 64
SGU_CHUNK = 128
SGU_GROUPS = 4

VMEM_LIMIT_BYTES = 56 * 1024 * 1024
TOKEN_TILE = 512
FFN_COL_CHUNK = 512


def _dot(a, b):
    return jnp.dot(a, b, preferred_element_type=F32)


def _rms(x, g, eps=NORM_EPS):
    return x * lax.rsqrt(jnp.mean(x * x, axis=-1, keepdims=True) + eps) * g


def _silu(x):
    return x * jax.nn.sigmoid(x)


def _gelu(x):
    return 0.5 * x * (1.0 + lax.erf(x * (0.5 ** 0.5)))


def _resident(shape):
    zeros = (0,) * len(shape)
    return pl.BlockSpec(shape, lambda *_: zeros, pipeline_mode=pl.Buffered(1))


def _hbm():
    return pl.BlockSpec(memory_space=pl.ANY)


def _prompt_rows(tile, width, n_prompt):
    return pl.BlockSpec((tile, width), lambda i: (jnp.minimum(i, n_prompt - 1), 0))


def _two_streams(body_prompt, body_sample, n_prompt):
    step = pl.program_id(0)
    pl.when(step < n_prompt)(body_prompt)
    pl.when(step == n_prompt)(body_sample)


_PARAMS = pltpu.CompilerParams(dimension_semantics=("arbitrary",),
                               vmem_limit_bytes=VMEM_LIMIT_BYTES)


def _convert_specs(sources, n_prompt):
    in_specs, out_specs, out_shape = [], [], []
    for w, lead in sources:
        rows, cols = w.shape[-2:]
        blk = rows // n_prompt
        assert blk * n_prompt == rows and blk % 16 == 0
        in_specs.append(pl.BlockSpec(
            (None,) * len(lead) + (blk, cols),
            lambda i, lead=lead: lead + (jnp.minimum(i, n_prompt - 1), 0)))
        out_specs.append(_prompt_rows(blk, cols, n_prompt))
        out_shape.append(jax.ShapeDtypeStruct((rows, cols), BF16))
    return in_specs, out_specs, out_shape


def _convert(src_refs, dst_refs):
    for src, dst in zip(src_refs, dst_refs):
        dst[...] = src[...].astype(BF16)


class _CastStream:
    def __init__(self, pieces, stage, sems):
        self.pieces, self.stage, self.sems = pieces, stage, sems

    def _copy(self, j):
        slot = j % 2
        return pltpu.make_async_copy(self.pieces[j][0], self.stage.at[slot], self.sems.at[slot])

    def start(self, j):
        if j < len(self.pieces):
            self._copy(j).start()

    def land(self, j):
        self._copy(j).wait()
        _, dst, idx = self.pieces[j]
        dst[idx] = self.stage[j % 2].astype(BF16)
        self.start(j + 2)


def _col_pieces(hbm, lead, dst, unit):
    n = dst.shape[1] // unit
    return [(hbm.at[lead + (slice(None), pl.ds(j * unit, unit))], dst,
             (slice(None), slice(j * unit, (j + 1) * unit))) for j in range(n)]


def _row_pieces(hbm, lead, dst, unit):
    n = dst.shape[0] // unit
    return [(hbm.at[lead + (pl.ds(j * unit, unit), slice(None))], dst,
             (slice(j * unit, (j + 1) * unit), slice(None))) for j in range(n)]


def _stage(shape):
    return [pltpu.VMEM((2,) + shape, F32), pltpu.SemaphoreType.DMA((2,))]


def _ffn_tile(x_ref, o_ref, pre_ref, post_ref, wg_ref, wu_ref, wd_ref, before_chunk=None):
    x = x_ref[...]
    xn = _rms(x, pre_ref[...]).astype(BF16)
    d_ff = wg_ref.shape[1]
    acc = jnp.zeros(x.shape, F32)
    for c in range(d_ff // FFN_COL_CHUNK):
        if before_chunk is not None:
            before_chunk(c)
        cols = slice(c * FFN_COL_CHUNK, (c + 1) * FFN_COL_CHUNK)
        h = _silu(_dot(xn, wg_ref[:, cols])) * _dot(xn, wu_ref[:, cols])
        acc = acc + _dot(h.astype(BF16), wd_ref[cols, :])
    o_ref[...] = x + 0.5 * _rms(acc, post_ref[...])


def _ffn_kernel(*refs, n_prompt, lead, n_convert):
    xp_ref, xs_ref, pre_ref, post_ref, wg_hbm, wu_hbm, wd_hbm = refs[:7]
    conv_src = refs[7:7 + n_convert]
    op_ref, os_ref = refs[7 + n_convert:9 + n_convert]
    conv_dst = refs[9 + n_convert:9 + 2 * n_convert]
    wg_ref, wu_ref, wd_ref, *dma = refs[9 + 2 * n_convert:]
    weights = (pre_ref, post_ref, wg_ref, wu_ref, wd_ref)
    n_chunks = wg_ref.shape[1] // FFN_COL_CHUNK

    _convert(conv_src, conv_dst)

    def first_f32():
        stage_g, sem_g, stage_u, sem_u, stage_d, sem_d = dma
        unit = stage_d.shape[1]
        per_chunk = FFN_COL_CHUNK // unit
        streams = [_CastStream(_col_pieces(wg_hbm, lead, wg_ref, unit), stage_g, sem_g),
                   _CastStream(_col_pieces(wu_hbm, lead, wu_ref, unit), stage_u, sem_u),
                   _CastStream(_row_pieces(wd_hbm, lead, wd_ref, unit), stage_d, sem_d)]
        for s in streams:
            s.start(0)
            s.start(1)

        def land(c):
            for j in range(c * per_chunk, (c + 1) * per_chunk):
                for s in streams:
                    s.land(j)

        _ffn_tile(xp_ref, op_ref, *weights, before_chunk=land)

    def first_bf16():
        (sems,) = dma

        def copies(c):
            cols = pl.ds(c * FFN_COL_CHUNK, FFN_COL_CHUNK)
            return [pltpu.make_async_copy(wg_hbm.at[:, cols], wg_ref.at[:, cols], sems.at[0, c]),
                    pltpu.make_async_copy(wu_hbm.at[:, cols], wu_ref.at[:, cols], sems.at[1, c]),
                    pltpu.make_async_copy(wd_hbm.at[cols, :], wd_ref.at[cols, :], sems.at[2, c])]

        for c in range(n_chunks):
            for cp in copies(c):
                cp.start()

        def land(c):
            for cp in copies(c):
                cp.wait()

        _ffn_tile(xp_ref, op_ref, *weights, before_chunk=land)

    step = pl.program_id(0)
    pl.when(step == 0)(first_bf16 if lead is None else first_f32)
    pl.when(jnp.logical_and(step > 0, step < n_prompt))(
        functools.partial(_ffn_tile, xp_ref, op_ref, *weights))
    pl.when(step == n_prompt)(functools.partial(_ffn_tile, xs_ref, os_ref, *weights))


def _ffn(xp, xs, pre, post, wg, wu, wd, lead, convert):
    mp, d = xp.shape
    ms = xs.shape[0]
    d_ff = wg.shape[-1]
    n_prompt = mp // TOKEN_TILE
    n_chunks = d_ff // FFN_COL_CHUNK
    rows_p = _prompt_rows(TOKEN_TILE, d, n_prompt)
    c_in, c_out, c_shape = _convert_specs(convert, n_prompt)
    if lead is None:
        dma = [pltpu.SemaphoreType.DMA((3, n_chunks))]
    else:
        unit = FFN_COL_CHUNK // 2
        dma = [*_stage((d, unit)), *_stage((d, unit)), *_stage((unit, d))]
    out = pl.pallas_call(
        functools.partial(_ffn_kernel, n_prompt=n_prompt, lead=lead, n_convert=len(convert)),
        grid=(n_prompt + 1,),
        in_specs=[rows_p, _resident((ms, d)), _resident((1, d)), _resident((1, d)),
                  _hbm(), _hbm(), _hbm(), *c_in],
        out_specs=[rows_p, pl.BlockSpec((ms, d), lambda i: (0, 0)), *c_out],
        out_shape=[jax.ShapeDtypeStruct((mp, d), F32), jax.ShapeDtypeStruct((ms, d), F32),
                   *c_shape],
        scratch_shapes=[pltpu.VMEM((d, d_ff), BF16), pltpu.VMEM((d, d_ff), BF16),
                        pltpu.VMEM((d_ff, d), BF16), *dma],
        compiler_params=_PARAMS,
        name="ffn",
    )(xp, xs, pre.reshape(1, d), post.reshape(1, d), wg, wu, wd, *[w for w, _ in convert])
    return out[0], out[1], out[2:]


def _gla_tile(x_ref, y_ref, s_ref, pre_ref, post_ref, win_ref, wg2_ref, bg_ref, gn_ref, wout_ref,
              q_s, k_s, v_s, g_s, o_s, xn_s, qg_s, kg_s, kdt_s, el_s, a_s, u_s, *, nb, nc, L):
    heads = GLA_HEADS
    key = wg2_ref.shape[1]
    val = wout_ref.shape[0]
    dk, dv = key // heads, val // heads
    r = nb * nc * L
    blk = slice(0, r)

    xn = _rms(x_ref[...], pre_ref[...]).astype(BF16)
    xn_s[blk, :] = xn
    q_s[blk, :] = _dot(xn, win_ref[:, 0:key]) * (dk ** -0.5)
    k_s[blk, :] = _dot(xn, win_ref[:, key:2 * key])
    v_s[blk, :] = _dot(xn, win_ref[:, 2 * key:2 * key + val]).astype(BF16)
    zg = _dot(xn, win_ref[:, 2 * key + 2 * val:])
    z = _dot(zg.astype(BF16), wg2_ref[...].astype(BF16)) + bg_ref[...]
    g_s[blk, :] = jax.nn.log_sigmoid(z) / GLA_TAU

    row = lax.broadcasted_iota(jnp.int32, (L, L), 0)
    col = lax.broadcasted_iota(jnp.int32, (L, L), 1)
    causal = row >= col
    tri = jnp.where(causal, 1.0, 0.0).astype(BF16)

    chunks = range(nb * nc)
    rows = [slice(i * L, (i + 1) * L) for i in chunks]
    kslc = [slice(h * dk, (h + 1) * dk) for h in range(heads)]
    vslc = [slice(h * dv, (h + 1) * dv) for h in range(heads)]

    for i in chunks:
        gc = g_s[rows[i], :]
        g1 = gc.astype(BF16)
        rem = gc - g1.astype(F32)
        g2 = rem.astype(BF16)
        g3 = (rem - g2.astype(F32)).astype(BF16)
        g_s[rows[i], :] = _dot(tri, g1) + _dot(tri, g2) + _dot(tri, g3)

    for i in chunks:
        b = g_s[rows[i], :]
        b_last = b[L - 1:L, :]
        qc = q_s[rows[i], :]
        kc = k_s[rows[i], :]
        qg_s[rows[i], :] = (qc * jnp.exp(b)).astype(BF16)
        kg_s[rows[i], :] = (kc * jnp.exp(-b)).astype(BF16)
        kdt_s[i, :, 0:L] = jnp.transpose(kc * jnp.exp(b_last - b)).astype(BF16)
        el_s[i] = jnp.broadcast_to(jnp.exp(b_last), el_s.shape[1:])

    for i in chunks:
        for h in range(heads):
            a = lax.dot_general(qg_s[rows[i], kslc[h]], kg_s[rows[i], kslc[h]],
                                (((1,), (1,)), ((), ())), preferred_element_type=F32)
            a_s[i * heads + h, 0:L, 0:L] = jnp.where(causal, a, 0.0).astype(BF16)

    for i in chunks:
        for h in range(heads):
            lhs = jnp.concatenate([a_s[i * heads + h, 0:L, 0:L], kdt_s[i, kslc[h], 0:L]], axis=0)
            res = _dot(lhs, v_s[rows[i], vslc[h]])
            o_s[rows[i], vslc[h]] = res[:L]
            u_s[i * heads + h] = res[L:]

    for i in chunks:
        bi = i // nc
        for h in range(heads):
            s_h = s_ref[bi, h]
            o_s[rows[i], vslc[h]] += _dot(qg_s[rows[i], kslc[h]], s_h.astype(BF16))
            decay = jnp.transpose(jnp.broadcast_to(el_s[i, 0:1, kslc[h]], (dk, dk)))
            s_ref[bi, h] = s_h * jnp.tile(decay, (1, dv // dk)) + u_s[i * heads + h]

    gate = _silu(_dot(xn_s[blk, :], win_ref[:, 2 * key + val:2 * key + 2 * val]))
    gn = gn_ref[...]
    o_n = jnp.concatenate([_rms(o_s[blk, vslc[h]], gn) for h in range(heads)], axis=1)
    y = _dot((o_n * gate).astype(BF16), wout_ref[...])
    y_ref[...] = x_ref[...] + _rms(y, post_ref[...])


def _gla_kernel(*refs, n_prompt, tiles_per_seq, chunk_p, batch_s, chunk_s, n_convert):
    xp_ref, xs_ref, s0_ref = refs[:3]
    weights = refs[3:10]
    conv_src = refs[10:10 + n_convert]
    yp_ref, ys_ref, sp_ref, ss_ref = refs[10 + n_convert:14 + n_convert]
    conv_dst = refs[14 + n_convert:14 + 2 * n_convert]
    scratch = refs[14 + 2 * n_convert:]

    _convert(conv_src, conv_dst)

    def prompt():
        @pl.when(pl.program_id(0) % tiles_per_seq == 0)
        def _():
            sp_ref[...] = jnp.zeros(sp_ref.shape, F32)

        _gla_tile(xp_ref, yp_ref, sp_ref, *weights, *scratch,
                  nb=1, nc=xp_ref.shape[0] // chunk_p, L=chunk_p)

    def sample():
        ss_ref[...] = s0_ref[...]
        _gla_tile(xs_ref, ys_ref, ss_ref, *weights, *scratch,
                  nb=batch_s, nc=xs_ref.shape[0] // (batch_s * chunk_s), L=chunk_s)

    _two_streams(prompt, sample, n_prompt)


def _gla(xp, xs, s0, pre, post, w_in, w_gate2, b_gate, g_norm, w_out, convert,
         *, batch_p, batch_s):
    mp, d = xp.shape
    ms = xs.shape[0]
    heads = GLA_HEADS
    key = w_gate2.shape[1]
    val = w_out.shape[0]
    seq_p, seq_s = mp // batch_p, ms // batch_s
    chunk_p, chunk_s = min(seq_p, GLA_CHUNK), min(seq_s, GLA_CHUNK)
    tile = TOKEN_TILE
    assert seq_p % tile == 0 and tile % chunk_p == 0 and seq_s % chunk_s == 0 and ms <= tile
    tiles_per_seq = seq_p // tile
    n_prompt = mp // tile
    n_chunks = max(tile // chunk_p, ms // chunk_s)
    chunk = max(chunk_p, chunk_s)
    state_shape = (heads, key // heads, val // heads)
    rows_p = _prompt_rows(tile, d, n_prompt)
    state_p = pl.BlockSpec((1,) + state_shape,
                           lambda i: (jnp.minimum(i, n_prompt - 1) // tiles_per_seq, 0, 0, 0))
    state_s = pl.BlockSpec((batch_s,) + state_shape, lambda i: (0, 0, 0, 0))
    c_in, c_out, c_shape = _convert_specs(convert, n_prompt)
    kern = functools.partial(_gla_kernel, n_prompt=n_prompt, tiles_per_seq=tiles_per_seq,
                             chunk_p=chunk_p, batch_s=batch_s, chunk_s=chunk_s,
                             n_convert=len(convert))
    out = pl.pallas_call(
        kern,
        grid=(n_prompt + 1,),
        in_specs=[rows_p, _resident((ms, d)), _resident((batch_s,) + state_shape),
                  _resident((1, d)), _resident((1, d)), _resident(w_in.shape),
                  _resident(w_gate2.shape), _resident((1, key)), _resident((1, g_norm.shape[0])),
                  _resident(w_out.shape), *c_in],
        out_specs=[rows_p, pl.BlockSpec((ms, d), lambda i: (0, 0)), state_p, state_s, *c_out],
        out_shape=[jax.ShapeDtypeStruct((mp, d), F32), jax.ShapeDtypeStruct((ms, d), F32),
                   jax.ShapeDtypeStruct((batch_p,) + state_shape, F32),
                   jax.ShapeDtypeStruct((batch_s,) + state_shape, F32), *c_shape],
        scratch_shapes=[pltpu.VMEM((tile, key), F32), pltpu.VMEM((tile, key), F32),
                        pltpu.VMEM((tile, val), BF16), pltpu.VMEM((tile, key), F32),
                        pltpu.VMEM((tile, val), F32), pltpu.VMEM((tile, d), BF16),
                        pltpu.VMEM((tile, key), BF16), pltpu.VMEM((tile, key), BF16),
                        pltpu.VMEM((n_chunks, key, chunk), BF16),
                        pltpu.VMEM((n_chunks, 8, key), F32),
                        pltpu.VMEM((n_chunks * heads, chunk, chunk), BF16),
                        pltpu.VMEM((n_chunks * heads,) + state_shape[1:], F32)],
        compiler_params=_PARAMS,
        name="gla",
    )(xp, xs, s0, pre.reshape(1, d), post.reshape(1, d), w_in, w_gate2, b_gate.reshape(1, key),
      g_norm.reshape(1, -1), w_out, *[w for w, _ in convert])
    return out[0], out[1], out[2], out[3], out[4:]


def _sgu_tile(x_ref, y_ref, v_ref, pre_ref, post_ref, win_ref, bin_ref, lng_ref, lnb_ref,
              wmix_ref, bmix_ref, wout_ref, v_s, *, L):
    groups = SGU_GROUPS
    half = wout_ref.shape[0]
    gd = half // groups
    r = x_ref.shape[0]
    blk = slice(0, r)
    win = wmix_ref.shape[1]

    x = x_ref[...]
    xn = _rms(x, pre_ref[...]).astype(BF16)

    s1 = jnp.zeros((r, 1), F32)
    for g in range(groups):
        seg = slice(half + g * gd, half + (g + 1) * gd)
        vg = _gelu(_dot(xn, win_ref[:, seg]) + bin_ref[:, seg])
        v_s[blk, g * gd:(g + 1) * gd] = vg
        s1 = s1 + jnp.sum(vg, axis=-1, keepdims=True)
    mu = s1 / half

    def u_proj(g):
        seg = slice(g * gd, (g + 1) * gd)
        return _gelu(_dot(xn, win_ref[:, seg]) + bin_ref[:, seg])

    u_next = u_proj(0)
    s2 = jnp.zeros((r, 1), F32)
    for g in range(groups):
        xc = v_s[blk, g * gd:(g + 1) * gd] - mu
        s2 = s2 + jnp.sum(xc * xc, axis=-1, keepdims=True)
    rstd = lax.rsqrt(s2 / half + LN_EPS)

    row = lax.broadcasted_iota(jnp.int32, (win, win), 0)
    col = lax.broadcasted_iota(jnp.int32, (win, win), 1)
    mask = row >= col
    if L < win:
        mask = jnp.logical_and(mask, row // L == col // L)

    acc = jnp.zeros(x.shape, F32)
    for g in range(groups):
        seg = slice(g * gd, (g + 1) * gd)
        vn = (v_s[blk, seg] - mu) * rstd * lng_ref[:, seg] + lnb_ref[:, seg]
        if v_ref is not None:
            v_ref[:, seg] = vn
        vnb = vn.astype(BF16)
        wm = jnp.where(mask, wmix_ref[g], 0.0).astype(BF16)
        bcol = bmix_ref[:, g:g + 1]
        mixed = jnp.concatenate(
            [_dot(wm, vnb[c * win:(c + 1) * win, :]) + bcol for c in range(r // win)], axis=0)
        u = u_next
        if g + 1 < groups:
            u_next = u_proj(g + 1)
        acc = acc + _dot((u * mixed).astype(BF16), wout_ref[seg, :])
    y_ref[...] = x + _rms(acc, post_ref[...])


def _sgu_kernel(*refs, n_prompt, chunk_p, chunk_s, n_convert):
    (xp_ref, xs_ref, pre_ref, post_ref, win_ref, bin_ref, lng_ref, lnb_ref,
     wmix_p_ref, bmix_p_ref, wmix_s_ref, bmix_s_ref, wout_ref) = refs[:13]
    conv_src = refs[13:13 + n_convert]
    yp_ref, ys_ref, vs_ref = refs[13 + n_convert:16 + n_convert]
    conv_dst = refs[16 + n_convert:16 + 2 * n_convert]
    (v_s,) = refs[16 + 2 * n_convert:]
    shared = (pre_ref, post_ref, win_ref, bin_ref, lng_ref, lnb_ref)

    _convert(conv_src, conv_dst)
    _two_streams(
        functools.partial(_sgu_tile, xp_ref, yp_ref, None, *shared, wmix_p_ref, bmix_p_ref,
                          wout_ref, v_s, L=chunk_p),
        functools.partial(_sgu_tile, xs_ref, ys_ref, vs_ref, *shared, wmix_s_ref, bmix_s_ref,
                          wout_ref, v_s, L=chunk_s),
        n_prompt)


def _sgu(xp, xs, pre, post, w_in, b_in, ln_g, ln_b, w_s, b_s, w_out, convert, *, seq_p, seq_s):
    mp, d = xp.shape
    ms = xs.shape[0]
    half = w_out.shape[0]
    win = w_s.shape[1]
    chunk_p, chunk_s = min(seq_p, win), min(seq_s, win)
    tile = TOKEN_TILE
    assert chunk_p == win and tile % win == 0 and seq_p % win == 0
    assert ms % win == 0 and win % chunk_s == 0 and seq_s == chunk_s and ms <= tile
    n_prompt = mp // tile
    reps = win // chunk_s
    wmix_s = jnp.tile(w_s[:, :chunk_s, :chunk_s], (1, reps, reps))
    bmix_s = jnp.tile(b_s[:, :chunk_s].T, (reps, 1))
    rows_p = _prompt_rows(tile, d, n_prompt)
    c_in, c_out, c_shape = _convert_specs(convert, n_prompt)
    kern = functools.partial(_sgu_kernel, n_prompt=n_prompt, chunk_p=chunk_p, chunk_s=chunk_s,
                             n_convert=len(convert))
    out = pl.pallas_call(
        kern,
        grid=(n_prompt + 1,),
        in_specs=[rows_p, _resident((ms, d)), _resident((1, d)), _resident((1, d)),
                  _resident(w_in.shape), _resident((1, 2 * half)), _resident((1, half)),
                  _resident((1, half)), _resident(w_s.shape), _resident((win, w_s.shape[0])),
                  _resident(w_s.shape), _resident((win, w_s.shape[0])), _resident(w_out.shape),
                  *c_in],
        out_specs=[rows_p, pl.BlockSpec((ms, d), lambda i: (0, 0)),
                   pl.BlockSpec((ms, half), lambda i: (0, 0)), *c_out],
        out_shape=[jax.ShapeDtypeStruct((mp, d), F32), jax.ShapeDtypeStruct((ms, d), F32),
                   jax.ShapeDtypeStruct((ms, half), F32), *c_shape],
        scratch_shapes=[pltpu.VMEM((tile, half), F32)],
        compiler_params=_PARAMS,
        name="sgu",
    )(xp, xs, pre.reshape(1, d), post.reshape(1, d), w_in, b_in.reshape(1, -1),
      ln_g.reshape(1, -1), ln_b.reshape(1, -1), w_s, b_s.T, wmix_s, bmix_s, w_out,
      *[w for w, _ in convert])
    return out[0], out[1], out[2], out[3:]


def kernel(x_prompt, x_sample, state_gla, norm_pre, norm_post, ffn_w_gate, ffn_w_up, ffn_w_down,
           gla_w_in, gla_w_gate2, gla_b_gate, gla_norm, gla_w_out, sgu_w_in, sgu_b_in, sgu_ln_g,
           sgu_ln_b, sgu_w_s, sgu_b_s, sgu_w_out):
    batch, seq, d = x_prompt.shape
    dbatch, dseq, _ = x_sample.shape
    depth = norm_pre.shape[0]
    xp = x_prompt.reshape(batch * seq, d)
    xs = x_sample.reshape(dbatch * dseq, d)
    gla_p, gla_s, sgu_s = [], [], []

    def ffn_sources(i, k):
        return [(ffn_w_gate, (i, k)), (ffn_w_up, (i, k)), (ffn_w_down, (i, k))]

    sources = []
    for i in range(depth):
        j = i // 2
        mixer = [(gla_w_in, (j,)), (gla_w_out, (j,))] if i % 2 == 0 else \
            [(sgu_w_in, (j,)), (sgu_w_out, (j,))]
        sources += [ffn_sources(i, 0), mixer, ffn_sources(i, 1)]
    sources.append([])

    ready = None
    for i in range(depth):
        j = i // 2
        for part in range(3):
            nxt = sources[3 * i + part + 1]
            if part != 1:
                k, n = (0, 0) if part == 0 else (1, 2)
                if ready is None:
                    w, lead = (ffn_w_gate, ffn_w_up, ffn_w_down), (i, k)
                else:
                    w, lead = ready, None
                xp, xs, ready = _ffn(xp, xs, norm_pre[i, n], norm_post[i, n], *w, lead, nxt)
            elif i % 2 == 0:
                w_in, w_out = ready
                xp, xs, sp, ss, ready = _gla(
                    xp, xs, state_gla[j], norm_pre[i, 1], norm_post[i, 1], w_in, gla_w_gate2[j],
                    gla_b_gate[j], gla_norm[j], w_out, nxt, batch_p=batch, batch_s=dbatch)
                gla_p.append(sp)
                gla_s.append(ss)
            else:
                w_in, w_out = ready
                xp, xs, vs, ready = _sgu(
                    xp, xs, norm_pre[i, 1], norm_post[i, 1], w_in, sgu_b_in[j], sgu_ln_g[j],
                    sgu_ln_b[j], sgu_w_s[j], sgu_b_s[j], w_out, nxt, seq_p=seq, seq_s=dseq)
                sgu_s.append(vs.reshape(dbatch, dseq, -1))

    return (xp.reshape(batch, seq, d), xs.reshape(dbatch, dseq, d),
            jnp.stack(gla_p), jnp.stack(gla_s), jnp.stack(sgu_s))
```

```python
import functools

import jax
import jax.numpy as jnp
from jax import lax
from jax.experimental import pallas as pl
from jax.experimental.pallas import tpu as pltpu

F32 = jnp.float32
BF16 = jnp.bfloat16

NORM_EPS = 1e-6
LN_EPS = 1e-5
GLA_HEADS = 4
GLA_TAU = 16.0
GLA_CHUNK = 64
SGU_CHUNK = 128
SGU_GROUPS = 4

VMEM_LIMIT_BYTES = 56 * 1024 * 1024
TOKEN_TILE = 512
FFN_COL_CHUNK = 512


def _dot(a, b):
    return jnp.dot(a, b, preferred_element_type=F32)


def _rms(x, g, eps=NORM_EPS):
    return x * lax.rsqrt(jnp.mean(x * x, axis=-1, keepdims=True) + eps) * g


def _silu(x):
    return x * jax.nn.sigmoid(x)


def _gelu(x):
    return 0.5 * x * (1.0 + lax.erf(x * (0.5 ** 0.5)))


def _resident(shape):
    zeros = (0,) * len(shape)
    return pl.BlockSpec(shape, lambda *_: zeros, pipeline_mode=pl.Buffered(1))


def _hbm():
    return pl.BlockSpec(memory_space=pl.ANY)


def _prompt_rows(tile, width, n_prompt):
    return pl.BlockSpec((tile, width), lambda i: (jnp.minimum(i, n_prompt - 1), 0))


def _two_streams(body_prompt, body_sample, n_prompt):
    step = pl.program_id(0)
    pl.when(step < n_prompt)(body_prompt)
    pl.when(step == n_prompt)(body_sample)


_PARAMS = pltpu.CompilerParams(dimension_semantics=("arbitrary",),
                               vmem_limit_bytes=VMEM_LIMIT_BYTES)


def _convert_specs(sources, n_prompt):
    in_specs, out_specs, out_shape = [], [], []
    for w, lead in sources:
        rows, cols = w.shape[-2:]
        blk = rows // n_prompt
        assert blk * n_prompt == rows and blk % 16 == 0
        in_specs.append(pl.BlockSpec(
            (None,) * len(lead) + (blk, cols),
            lambda i, lead=lead: lead + (jnp.minimum(i, n_prompt - 1), 0)))
        out_specs.append(_prompt_rows(blk, cols, n_prompt))
        out_shape.append(jax.ShapeDtypeStruct((rows, cols), BF16))
    return in_specs, out_specs, out_shape


def _convert(src_refs, dst_refs):
    for src, dst in zip(src_refs, dst_refs):
        dst[...] = src[...].astype(BF16)


class _CastStream:
    def __init__(self, pieces, stage, sems):
        self.pieces, self.stage, self.sems = pieces, stage, sems

    def _copy(self, j):
        slot = j % 2
        return pltpu.make_async_copy(self.pieces[j][0], self.stage.at[slot], self.sems.at[slot])

    def start(self, j):
        if j < len(self.pieces):
            self._copy(j).start()

    def land(self, j):
        self._copy(j).wait()
        _, dst, idx = self.pieces[j]
        dst[idx] = self.stage[j % 2].astype(BF16)
        self.start(j + 2)


def _col_pieces(hbm, lead, dst, unit):
    n = dst.shape[1] // unit
    return [(hbm.at[lead + (slice(None), pl.ds(j * unit, unit))], dst,
             (slice(None), slice(j * unit, (j + 1) * unit))) for j in range(n)]


def _row_pieces(hbm, lead, dst, unit):
    n = dst.shape[0] // unit
    return [(hbm.at[lead + (pl.ds(j * unit, unit), slice(None))], dst,
             (slice(j * unit, (j + 1) * unit), slice(None))) for j in range(n)]


def _stage(shape):
    return [pltpu.VMEM((2,) + shape, F32), pltpu.SemaphoreType.DMA((2,))]


def _ffn_tile(x_ref, o_ref, pre_ref, post_ref, wg_ref, wu_ref, wd_ref, before_chunk=None):
    x = x_ref[...]
    xn = _rms(x, pre_ref[...]).astype(BF16)
    d_ff = wg_ref.shape[1]
    acc = jnp.zeros(x.shape, F32)
    for c in range(d_ff // FFN_COL_CHUNK):
        if before_chunk is not None:
            before_chunk(c)
        cols = slice(c * FFN_COL_CHUNK, (c + 1) * FFN_COL_CHUNK)
        h = _silu(_dot(xn, wg_ref[:, cols])) * _dot(xn, wu_ref[:, cols])
        acc = acc + _dot(h.astype(BF16), wd_ref[cols, :])
    o_ref[...] = x + 0.5 * _rms(acc, post_ref[...])


def _ffn_kernel(*refs, n_prompt, lead, n_convert, mode):
    xp_ref, xs_ref, pre_ref, post_ref, wg_hbm, wu_hbm, wd_hbm = refs[:7]
    conv_src = refs[7:7 + n_convert]
    op_ref, os_ref = refs[7 + n_convert:9 + n_convert]
    conv_dst = refs[9 + n_convert:9 + 2 * n_convert]
    if mode == "resident":
        _convert(conv_src, conv_dst)
        weights = (pre_ref, post_ref, wg_hbm, wu_hbm, wd_hbm)
        _two_streams(functools.partial(_ffn_tile, xp_ref, op_ref, *weights),
                     functools.partial(_ffn_tile, xs_ref, os_ref, *weights), n_prompt)
        return
    wg_ref, wu_ref, wd_ref, *dma = refs[9 + 2 * n_convert:]
    weights = (pre_ref, post_ref, wg_ref, wu_ref, wd_ref)
    n_chunks = wg_ref.shape[1] // FFN_COL_CHUNK

    _convert(conv_src, conv_dst)

    def first_f32():
        stage_g, sem_g, stage_u, sem_u, stage_d, sem_d = dma
        unit = stage_d.shape[1]
        per_chunk = FFN_COL_CHUNK // unit
        streams = [_CastStream(_col_pieces(wg_hbm, lead, wg_ref, unit), stage_g, sem_g),
                   _CastStream(_col_pieces(wu_hbm, lead, wu_ref, unit), stage_u, sem_u),
                   _CastStream(_row_pieces(wd_hbm, lead, wd_ref, unit), stage_d, sem_d)]
        for s in streams:
            s.start(0)
            s.start(1)

        def land(c):
            for j in range(c * per_chunk, (c + 1) * per_chunk):
                for s in streams:
                    s.land(j)

        _ffn_tile(xp_ref, op_ref, *weights, before_chunk=land)

    def first_bf16():
        (sems,) = dma

        def copies(c):
            cols = pl.ds(c * FFN_COL_CHUNK, FFN_COL_CHUNK)
            return [pltpu.make_async_copy(wg_hbm.at[:, cols], wg_ref.at[:, cols], sems.at[0, c]),
                    pltpu.make_async_copy(wu_hbm.at[:, cols], wu_ref.at[:, cols], sems.at[1, c]),
                    pltpu.make_async_copy(wd_hbm.at[cols, :], wd_ref.at[cols, :], sems.at[2, c])]

        for c in range(n_chunks):
            for cp in copies(c):
                cp.start()

        def land(c):
            for cp in copies(c):
                cp.wait()

        if mode == "upfront":
            for c in range(n_chunks):
                land(c)
            _ffn_tile(xp_ref, op_ref, *weights)
        else:
            _ffn_tile(xp_ref, op_ref, *weights, before_chunk=land)

    step = pl.program_id(0)
    pl.when(step == 0)(first_bf16 if lead is None else first_f32)
    pl.when(jnp.logical_and(step > 0, step < n_prompt))(
        functools.partial(_ffn_tile, xp_ref, op_ref, *weights))
    pl.when(step == n_prompt)(functools.partial(_ffn_tile, xs_ref, os_ref, *weights))


def _ffn(xp, xs, pre, post, wg, wu, wd, lead, convert, mode="chunks"):
    mp, d = xp.shape
    ms = xs.shape[0]
    d_ff = wg.shape[-1]
    n_prompt = mp // TOKEN_TILE
    n_chunks = d_ff // FFN_COL_CHUNK
    rows_p = _prompt_rows(TOKEN_TILE, d, n_prompt)
    c_in, c_out, c_shape = _convert_specs(convert, n_prompt)
    if lead is None:
        dma = [pltpu.SemaphoreType.DMA((3, n_chunks))]
    else:
        unit = FFN_COL_CHUNK // 2
        dma = [*_stage((d, unit)), *_stage((d, unit)), *_stage((unit, d))]
    if mode == "resident":
        w_specs = [_resident(wg.shape), _resident(wu.shape), _resident(wd.shape)]
        scratch = []
    else:
        w_specs = [_hbm(), _hbm(), _hbm()]
        scratch = [pltpu.VMEM((d, d_ff), BF16), pltpu.VMEM((d, d_ff), BF16),
                   pltpu.VMEM((d_ff, d), BF16), *dma]
    out = pl.pallas_call(
        functools.partial(_ffn_kernel, n_prompt=n_prompt, lead=lead, n_convert=len(convert),
                          mode=mode),
        grid=(n_prompt + 1,),
        in_specs=[rows_p, _resident((ms, d)), _resident((1, d)), _resident((1, d)),
                  *w_specs, *c_in],
        out_specs=[rows_p, pl.BlockSpec((ms, d), lambda i: (0, 0)), *c_out],
        out_shape=[jax.ShapeDtypeStruct((mp, d), F32), jax.ShapeDtypeStruct((ms, d), F32),
                   *c_shape],
        scratch_shapes=scratch,
        compiler_params=_PARAMS,
        name="ffn",
    )(xp, xs, pre.reshape(1, d), post.reshape(1, d), wg, wu, wd, *[w for w, _ in convert])
    return out[0], out[1], out[2:]


def _gla_tile(x_ref, y_ref, s_ref, pre_ref, post_ref, win_ref, wg2_ref, bg_ref, gn_ref, wout_ref,
              q_s, k_s, v_s, g_s, o_s, xn_s, qg_s, kg_s, kdt_s, el_s, a_s, u_s, *, nb, nc, L):
    heads = GLA_HEADS
    key = wg2_ref.shape[1]
    val = wout_ref.shape[0]
    dk, dv = key // heads, val // heads
    r = nb * nc * L
    blk = slice(0, r)

    xn = _rms(x_ref[...], pre_ref[...]).astype(BF16)
    xn_s[blk, :] = xn
    q_s[blk, :] = _dot(xn, win_ref[:, 0:key]) * (dk ** -0.5)
    k_s[blk, :] = _dot(xn, win_ref[:, key:2 * key])
    v_s[blk, :] = _dot(xn, win_ref[:, 2 * key:2 * key + val]).astype(BF16)
    zg = _dot(xn, win_ref[:, 2 * key + 2 * val:])
    z = _dot(zg.astype(BF16), wg2_ref[...].astype(BF16)) + bg_ref[...]
    g_s[blk, :] = jax.nn.log_sigmoid(z) / GLA_TAU

    row = lax.broadcasted_iota(jnp.int32, (L, L), 0)
    col = lax.broadcasted_iota(jnp.int32, (L, L), 1)
    causal = row >= col
    tri = jnp.where(causal, 1.0, 0.0).astype(BF16)

    chunks = range(nb * nc)
    rows = [slice(i * L, (i + 1) * L) for i in chunks]
    kslc = [slice(h * dk, (h + 1) * dk) for h in range(heads)]
    vslc = [slice(h * dv, (h + 1) * dv) for h in range(heads)]

    for i in chunks:
        gc = g_s[rows[i], :]
        g1 = gc.astype(BF16)
        rem = gc - g1.astype(F32)
        g2 = rem.astype(BF16)
        g3 = (rem - g2.astype(F32)).astype(BF16)
        g_s[rows[i], :] = _dot(tri, g1) + _dot(tri, g2) + _dot(tri, g3)

    for i in chunks:
        b = g_s[rows[i], :]
        b_last = b[L - 1:L, :]
        qc = q_s[rows[i], :]
        kc = k_s[rows[i], :]
        qg_s[rows[i], :] = (qc * jnp.exp(b)).astype(BF16)
        kg_s[rows[i], :] = (kc * jnp.exp(-b)).astype(BF16)
        kdt_s[i, :, 0:L] = jnp.transpose(kc * jnp.exp(b_last - b)).astype(BF16)
        el_s[i] = jnp.broadcast_to(jnp.exp(b_last), el_s.shape[1:])

    for i in chunks:
        for h in range(heads):
            a = lax.dot_general(qg_s[rows[i], kslc[h]], kg_s[rows[i], kslc[h]],
                                (((1,), (1,)), ((), ())), preferred_element_type=F32)
            a_s[i * heads + h, 0:L, 0:L] = jnp.where(causal, a, 0.0).astype(BF16)

    for i in chunks:
        for h in range(heads):
            lhs = jnp.concatenate([a_s[i * heads + h, 0:L, 0:L], kdt_s[i, kslc[h], 0:L]], axis=0)
            res = _dot(lhs, v_s[rows[i], vslc[h]])
            o_s[rows[i], vslc[h]] = res[:L]
            u_s[i * heads + h] = res[L:]

    for i in chunks:
        bi = i // nc
        for h in range(heads):
            s_h = s_ref[bi, h]
            o_s[rows[i], vslc[h]] += _dot(qg_s[rows[i], kslc[h]], s_h.astype(BF16))
            decay = jnp.transpose(jnp.broadcast_to(el_s[i, 0:1, kslc[h]], (dk, dk)))
            s_ref[bi, h] = s_h * jnp.tile(decay, (1, dv // dk)) + u_s[i * heads + h]

    gate = _silu(_dot(xn_s[blk, :], win_ref[:, 2 * key + val:2 * key + 2 * val]))
    gn = gn_ref[...]
    o_n = jnp.concatenate([_rms(o_s[blk, vslc[h]], gn) for h in range(heads)], axis=1)
    y = _dot((o_n * gate).astype(BF16), wout_ref[...])
    y_ref[...] = x_ref[...] + _rms(y, post_ref[...])


def _gla_kernel(*refs, n_prompt, tiles_per_seq, chunk_p, batch_s, chunk_s, n_convert):
    xp_ref, xs_ref, s0_ref = refs[:3]
    weights = refs[3:10]
    conv_src = refs[10:10 + n_convert]
    yp_ref, ys_ref, sp_ref, ss_ref = refs[10 + n_convert:14 + n_convert]
    conv_dst = refs[14 + n_convert:14 + 2 * n_convert]
    scratch = refs[14 + 2 * n_convert:]

    _convert(conv_src, conv_dst)

    def prompt():
        @pl.when(pl.program_id(0) % tiles_per_seq == 0)
        def _():
            sp_ref[...] = jnp.zeros(sp_ref.shape, F32)

        _gla_tile(xp_ref, yp_ref, sp_ref, *weights, *scratch,
                  nb=1, nc=xp_ref.shape[0] // chunk_p, L=chunk_p)

    def sample():
        ss_ref[...] = s0_ref[...]
        _gla_tile(xs_ref, ys_ref, ss_ref, *weights, *scratch,
                  nb=batch_s, nc=xs_ref.shape[0] // (batch_s * chunk_s), L=chunk_s)

    _two_streams(prompt, sample, n_prompt)


def _gla(xp, xs, s0, pre, post, w_in, w_gate2, b_gate, g_norm, w_out, convert,
         *, batch_p, batch_s):
    mp, d = xp.shape
    ms = xs.shape[0]
    heads = GLA_HEADS
    key = w_gate2.shape[1]
    val = w_out.shape[0]
    seq_p, seq_s = mp // batch_p, ms // batch_s
    chunk_p, chunk_s = min(seq_p, GLA_CHUNK), min(seq_s, GLA_CHUNK)
    tile = TOKEN_TILE
    assert seq_p % tile == 0 and tile % chunk_p == 0 and seq_s % chunk_s == 0 and ms <= tile
    tiles_per_seq = seq_p // tile
    n_prompt = mp // tile
    n_chunks = max(tile // chunk_p, ms // chunk_s)
    chunk = max(chunk_p, chunk_s)
    state_shape = (heads, key // heads, val // heads)
    rows_p = _prompt_rows(tile, d, n_prompt)
    state_p = pl.BlockSpec((1,) + state_shape,
                           lambda i: (jnp.minimum(i, n_prompt - 1) // tiles_per_seq, 0, 0, 0))
    state_s = pl.BlockSpec((batch_s,) + state_shape, lambda i: (0, 0, 0, 0))
    c_in, c_out, c_shape = _convert_specs(convert, n_prompt)
    kern = functools.partial(_gla_kernel, n_prompt=n_prompt, tiles_per_seq=tiles_per_seq,
                             chunk_p=chunk_p, batch_s=batch_s, chunk_s=chunk_s,
                             n_convert=len(convert))
    out = pl.pallas_call(
        kern,
        grid=(n_prompt + 1,),
        in_specs=[rows_p, _resident((ms, d)), _resident((batch_s,) + state_shape),
                  _resident((1, d)), _resident((1, d)), _resident(w_in.shape),
                  _resident(w_gate2.shape), _resident((1, key)), _resident((1, g_norm.shape[0])),
                  _resident(w_out.shape), *c_in],
        out_specs=[rows_p, pl.BlockSpec((ms, d), lambda i: (0, 0)), state_p, state_s, *c_out],
        out_shape=[jax.ShapeDtypeStruct((mp, d), F32), jax.ShapeDtypeStruct((ms, d), F32),
                   jax.ShapeDtypeStruct((batch_p,) + state_shape, F32),
                   jax.ShapeDtypeStruct((batch_s,) + state_shape, F32), *c_shape],
        scratch_shapes=[pltpu.VMEM((tile, key), F32), pltpu.VMEM((tile, key), F32),
                        pltpu.VMEM((tile, val), BF16), pltpu.VMEM((tile, key), F32),
                        pltpu.VMEM((tile, val), F32), pltpu.VMEM((tile, d), BF16),
                        pltpu.VMEM((tile, key), BF16), pltpu.VMEM((tile, key), BF16),
                        pltpu.VMEM((n_chunks, key, chunk), BF16),
                        pltpu.VMEM((n_chunks, 8, key), F32),
                        pltpu.VMEM((n_chunks * heads, chunk, chunk), BF16),
                        pltpu.VMEM((n_chunks * heads,) + state_shape[1:], F32)],
        compiler_params=_PARAMS,
        name="gla",
    )(xp, xs, s0, pre.reshape(1, d), post.reshape(1, d), w_in, w_gate2, b_gate.reshape(1, key),
      g_norm.reshape(1, -1), w_out, *[w for w, _ in convert])
    return out[0], out[1], out[2], out[3], out[4:]


def _sgu_tile(x_ref, y_ref, v_ref, pre_ref, post_ref, win_ref, bin_ref, lng_ref, lnb_ref,
              wmix_ref, bmix_ref, wout_ref, v_s, *, L):
    groups = SGU_GROUPS
    half = wout_ref.shape[0]
    gd = half // groups
    r = x_ref.shape[0]
    blk = slice(0, r)
    win = wmix_ref.shape[1]

    x = x_ref[...]
    xn = _rms(x, pre_ref[...]).astype(BF16)

    s1 = jnp.zeros((r, 1), F32)
    for g in range(groups):
        seg = slice(half + g * gd, half + (g + 1) * gd)
        vg = _gelu(_dot(xn, win_ref[:, seg]) + bin_ref[:, seg])
        v_s[blk, g * gd:(g + 1) * gd] = vg
        s1 = s1 + jnp.sum(vg, axis=-1, keepdims=True)
    mu = s1 / half

    def u_proj(g):
        seg = slice(g * gd, (g + 1) * gd)
        return _gelu(_dot(xn, win_ref[:, seg]) + bin_ref[:, seg])

    u_next = u_proj(0)
    s2 = jnp.zeros((r, 1), F32)
    for g in range(groups):
        xc = v_s[blk, g * gd:(g + 1) * gd] - mu
        s2 = s2 + jnp.sum(xc * xc, axis=-1, keepdims=True)
    rstd = lax.rsqrt(s2 / half + LN_EPS)

    row = lax.broadcasted_iota(jnp.int32, (win, win), 0)
    col = lax.broadcasted_iota(jnp.int32, (win, win), 1)
    mask = row >= col
    if L < win:
        mask = jnp.logical_and(mask, row // L == col // L)

    acc = jnp.zeros(x.shape, F32)
    for g in range(groups):
        seg = slice(g * gd, (g + 1) * gd)
        vn = (v_s[blk, seg] - mu) * rstd * lng_ref[:, seg] + lnb_ref[:, seg]
        if v_ref is not None:
            v_ref[:, seg] = vn
        vnb = vn.astype(BF16)
        wm = jnp.where(mask, wmix_ref[g], 0.0).astype(BF16)
        bcol = bmix_ref[:, g:g + 1]
        mixed = jnp.concatenate(
            [_dot(wm, vnb[c * win:(c + 1) * win, :]) + bcol for c in range(r // win)], axis=0)
        u = u_next
        if g + 1 < groups:
            u_next = u_proj(g + 1)
        acc = acc + _dot((u * mixed).astype(BF16), wout_ref[seg, :])
    y_ref[...] = x + _rms(acc, post_ref[...])


def _sgu_kernel(*refs, n_prompt, chunk_p, chunk_s, n_convert):
    (xp_ref, xs_ref, pre_ref, post_ref, win_ref, bin_ref, lng_ref, lnb_ref,
     wmix_p_ref, bmix_p_ref, wmix_s_ref, bmix_s_ref, wout_ref) = refs[:13]
    conv_src = refs[13:13 + n_convert]
    yp_ref, ys_ref, vs_ref = refs[13 + n_convert:16 + n_convert]
    conv_dst = refs[16 + n_convert:16 + 2 * n_convert]
    (v_s,) = refs[16 + 2 * n_convert:]
    shared = (pre_ref, post_ref, win_ref, bin_ref, lng_ref, lnb_ref)

    _convert(conv_src, conv_dst)
    _two_streams(
        functools.partial(_sgu_tile, xp_ref, yp_ref, None, *shared, wmix_p_ref, bmix_p_ref,
                          wout_ref, v_s, L=chunk_p),
        functools.partial(_sgu_tile, xs_ref, ys_ref, vs_ref, *shared, wmix_s_ref, bmix_s_ref,
                          wout_ref, v_s, L=chunk_s),
        n_prompt)


def _sgu(xp, xs, pre, post, w_in, b_in, ln_g, ln_b, w_s, b_s, w_out, convert, *, seq_p, seq_s):
    mp, d = xp.shape
    ms = xs.shape[0]
    half = w_out.shape[0]
    win = w_s.shape[1]
    chunk_p, chunk_s = min(seq_p, win), min(seq_s, win)
    tile = TOKEN_TILE
    assert chunk_p == win and tile % win == 0 and seq_p % win == 0
    assert ms % win == 0 and win % chunk_s == 0 and seq_s == chunk_s and ms <= tile
    n_prompt = mp // tile
    reps = win // chunk_s
    wmix_s = jnp.tile(w_s[:, :chunk_s, :chunk_s], (1, reps, reps))
    bmix_s = jnp.tile(b_s[:, :chunk_s].T, (reps, 1))
    rows_p = _prompt_rows(tile, d, n_prompt)
    c_in, c_out, c_shape = _convert_specs(convert, n_prompt)
    kern = functools.partial(_sgu_kernel, n_prompt=n_prompt, chunk_p=chunk_p, chunk_s=chunk_s,
                             n_convert=len(convert))
    out = pl.pallas_call(
        kern,
        grid=(n_prompt + 1,),
        in_specs=[rows_p, _resident((ms, d)), _resident((1, d)), _resident((1, d)),
                  _resident(w_in.shape), _resident((1, 2 * half)), _resident((1, half)),
                  _resident((1, half)), _resident(w_s.shape), _resident((win, w_s.shape[0])),
                  _resident(w_s.shape), _resident((win, w_s.shape[0])), _resident(w_out.shape),
                  *c_in],
        out_specs=[rows_p, pl.BlockSpec((ms, d), lambda i: (0, 0)),
                   pl.BlockSpec((ms, half), lambda i: (0, 0)), *c_out],
        out_shape=[jax.ShapeDtypeStruct((mp, d), F32), jax.ShapeDtypeStruct((ms, d), F32),
                   jax.ShapeDtypeStruct((ms, half), F32), *c_shape],
        scratch_shapes=[pltpu.VMEM((tile, half), F32)],
        compiler_params=_PARAMS,
        name="sgu",
    )(xp, xs, pre.reshape(1, d), post.reshape(1, d), w_in, b_in.reshape(1, -1),
      ln_g.reshape(1, -1), ln_b.reshape(1, -1), w_s, b_s.T, wmix_s, bmix_s, w_out,
      *[w for w, _ in convert])
    return out[0], out[1], out[2], out[3:]


def kernel(x_prompt, x_sample, state_gla, norm_pre, norm_post, ffn_w_gate, ffn_w_up, ffn_w_down,
           gla_w_in, gla_w_gate2, gla_b_gate, gla_norm, gla_w_out, sgu_w_in, sgu_b_in, sgu_ln_g,
           sgu_ln_b, sgu_w_s, sgu_b_s, sgu_w_out):
    batch, seq, d = x_prompt.shape
    dbatch, dseq, _ = x_sample.shape
    depth = norm_pre.shape[0]
    xp = x_prompt.reshape(batch * seq, d)
    xs = x_sample.reshape(dbatch * dseq, d)
    gla_p, gla_s, sgu_s = [], [], []

    def ffn_sources(i, k):
        return [(ffn_w_gate, (i, k)), (ffn_w_up, (i, k)), (ffn_w_down, (i, k))]

    sources = []
    for i in range(depth):
        j = i // 2
        mixer = [(gla_w_in, (j,)), (gla_w_out, (j,))] if i % 2 == 0 else \
            [(sgu_w_in, (j,)), (sgu_w_out, (j,))]
        sources += [ffn_sources(i, 0), mixer, ffn_sources(i, 1)]
    sources.append([])

    ready = None
    for i in range(depth):
        j = i // 2
        for part in range(3):
            nxt = sources[3 * i + part + 1]
            if part != 1:
                k, n = (0, 0) if part == 0 else (1, 2)
                if ready is None:
                    w, lead = (ffn_w_gate, ffn_w_up, ffn_w_down), (i, k)
                else:
                    w, lead = ready, None
                mode = ("chunks", "chunks", "resident", "upfront")[2 * i + k]
                xp, xs, ready = _ffn(xp, xs, norm_pre[i, n], norm_post[i, n], *w, lead, nxt, mode)
            elif i % 2 == 0:
                w_in, w_out = ready
                xp, xs, sp, ss, ready = _gla(
                    xp, xs, state_gla[j], norm_pre[i, 1], norm_post[i, 1], w_in, gla_w_gate2[j],
                    gla_b_gate[j], gla_norm[j], w_out, nxt, batch_p=batch, batch_s=dbatch)
                gla_p.append(sp)
                gla_s.append(ss)
            else:
                w_in, w_out = ready
                xp, xs, vs, ready = _sgu(
                    xp, xs, norm_pre[i, 1], norm_post[i, 1], w_in, sgu_b_in[j], sgu_ln_g[j],
                    sgu_ln_b[j], sgu_w_s[j], sgu_b_s[j], w_out, nxt, seq_p=seq, seq_s=dseq)
                sgu_s.append(vs.reshape(dbatch, dseq, -1))

    return (xp.reshape(batch, seq, d), xs.reshape(dbatch, dseq, d),
            jnp.stack(gla_p), jnp.stack(gla_s), jnp.stack(sgu_s))
```

```python
import functools

import jax
import jax.numpy as jnp
from jax import lax
from jax.experimental import pallas as pl
from jax.experimental.pallas import tpu as pltpu

F32 = jnp.float32
BF16 = jnp.bfloat16

NORM_EPS = 1e-6
LN_EPS = 1e-5
GLA_HEADS = 4
GLA_TAU = 16.0
GLA_CHUNK = 64
SGU_CHUNK = 128
SGU_GROUPS = 4

VMEM_LIMIT_BYTES = 56 * 1024 * 1024
TOKEN_TILE = 512
FFN_COL_CHUNK = 512
FFN_CHUNKS_IN_FLIGHT = 2


def _dot(a, b):
    return jnp.dot(a, b, preferred_element_type=F32)


def _rms(x, g, eps=NORM_EPS):
    return x * lax.rsqrt(jnp.mean(x * x, axis=-1, keepdims=True) + eps) * g


def _silu(x):
    return x * jax.nn.sigmoid(x)


def _gelu(x):
    return 0.5 * x * (1.0 + lax.erf(x * (0.5 ** 0.5)))


def _resident(shape):
    zeros = (0,) * len(shape)
    return pl.BlockSpec(shape, lambda *_: zeros, pipeline_mode=pl.Buffered(1))


def _hbm():
    return pl.BlockSpec(memory_space=pl.ANY)


def _prompt_rows(tile, width, n_prompt):
    return pl.BlockSpec((tile, width), lambda i: (jnp.minimum(i, n_prompt - 1), 0))


def _two_streams(body_prompt, body_sample, n_prompt):
    step = pl.program_id(0)
    pl.when(step < n_prompt)(body_prompt)
    pl.when(step == n_prompt)(body_sample)


_PARAMS = pltpu.CompilerParams(dimension_semantics=("arbitrary",),
                               vmem_limit_bytes=VMEM_LIMIT_BYTES)


def _convert_specs(sources, n_prompt):
    in_specs, out_specs, out_shape = [], [], []
    for w, lead in sources:
        rows, cols = w.shape[-2:]
        blk = rows // n_prompt
        assert blk * n_prompt == rows and blk % 16 == 0
        in_specs.append(pl.BlockSpec(
            (None,) * len(lead) + (blk, cols),
            lambda i, lead=lead: lead + (jnp.minimum(i, n_prompt - 1), 0)))
        out_specs.append(_prompt_rows(blk, cols, n_prompt))
        out_shape.append(jax.ShapeDtypeStruct((rows, cols), BF16))
    return in_specs, out_specs, out_shape


def _convert(src_refs, dst_refs):
    for src, dst in zip(src_refs, dst_refs):
        dst[...] = src[...].astype(BF16)


class _CastStream:
    def __init__(self, pieces, stage, sems):
        self.pieces, self.stage, self.sems = pieces, stage, sems

    def _copy(self, j):
        slot = j % 2
        return pltpu.make_async_copy(self.pieces[j][0], self.stage.at[slot], self.sems.at[slot])

    def start(self, j):
        if j < len(self.pieces):
            self._copy(j).start()

    def land(self, j):
        self._copy(j).wait()
        _, dst, idx = self.pieces[j]
        dst[idx] = self.stage[j % 2].astype(BF16)
        self.start(j + 2)


def _col_pieces(hbm, lead, dst, unit):
    n = dst.shape[1] // unit
    return [(hbm.at[lead + (slice(None), pl.ds(j * unit, unit))], dst,
             (slice(None), slice(j * unit, (j + 1) * unit))) for j in range(n)]


def _row_pieces(hbm, lead, dst, unit):
    n = dst.shape[0] // unit
    return [(hbm.at[lead + (pl.ds(j * unit, unit), slice(None))], dst,
             (slice(j * unit, (j + 1) * unit), slice(None))) for j in range(n)]


def _stage(shape):
    return [pltpu.VMEM((2,) + shape, F32), pltpu.SemaphoreType.DMA((2,))]


def _ffn_tile(x_ref, o_ref, pre_ref, post_ref, wg_ref, wu_ref, wd_ref, before_chunk=None):
    x = x_ref[...]
    xn = _rms(x, pre_ref[...]).astype(BF16)
    d_ff = wg_ref.shape[1]
    acc = jnp.zeros(x.shape, F32)
    for c in range(d_ff // FFN_COL_CHUNK):
        if before_chunk is not None:
            before_chunk(c)
        cols = slice(c * FFN_COL_CHUNK, (c + 1) * FFN_COL_CHUNK)
        h = _silu(_dot(xn, wg_ref[:, cols])) * _dot(xn, wu_ref[:, cols])
        acc = acc + _dot(h.astype(BF16), wd_ref[cols, :])
    o_ref[...] = x + 0.5 * _rms(acc, post_ref[...])


def _ffn_kernel(*refs, n_prompt, lead, n_convert):
    xp_ref, xs_ref, pre_ref, post_ref, wg_hbm, wu_hbm, wd_hbm = refs[:7]
    conv_src = refs[7:7 + n_convert]
    op_ref, os_ref = refs[7 + n_convert:9 + n_convert]
    conv_dst = refs[9 + n_convert:9 + 2 * n_convert]
    wg_ref, wu_ref, wd_ref, *dma = refs[9 + 2 * n_convert:]
    weights = (pre_ref, post_ref, wg_ref, wu_ref, wd_ref)
    n_chunks = wg_ref.shape[1] // FFN_COL_CHUNK

    _convert(conv_src, conv_dst)

    def first_f32():
        stage_g, sem_g, stage_u, sem_u, stage_d, sem_d = dma
        unit = stage_d.shape[1]
        per_chunk = FFN_COL_CHUNK // unit
        streams = [_CastStream(_col_pieces(wg_hbm, lead, wg_ref, unit), stage_g, sem_g),
                   _CastStream(_col_pieces(wu_hbm, lead, wu_ref, unit), stage_u, sem_u),
                   _CastStream(_row_pieces(wd_hbm, lead, wd_ref, unit), stage_d, sem_d)]
        for s in streams:
            s.start(0)
            s.start(1)

        def land(c):
            for j in range(c * per_chunk, (c + 1) * per_chunk):
                for s in streams:
                    s.land(j)

        _ffn_tile(xp_ref, op_ref, *weights, before_chunk=land)

    def first_bf16():
        (sems,) = dma

        def copies(c):
            cols = pl.ds(c * FFN_COL_CHUNK, FFN_COL_CHUNK)
            return [pltpu.make_async_copy(wg_hbm.at[:, cols], wg_ref.at[:, cols], sems.at[0, c]),
                    pltpu.make_async_copy(wu_hbm.at[:, cols], wu_ref.at[:, cols], sems.at[1, c]),
                    pltpu.make_async_copy(wd_hbm.at[cols, :], wd_ref.at[cols, :], sems.at[2, c])]

        def start(c):
            if c < n_chunks:
                for cp in copies(c):
                    cp.start()

        for c in range(FFN_CHUNKS_IN_FLIGHT):
            start(c)

        def land(c):
            for cp in copies(c):
                cp.wait()
            start(c + FFN_CHUNKS_IN_FLIGHT)

        _ffn_tile(xp_ref, op_ref, *weights, before_chunk=land)

    step = pl.program_id(0)
    pl.when(step == 0)(first_bf16 if lead is None else first_f32)
    pl.when(jnp.logical_and(step > 0, step < n_prompt))(
        functools.partial(_ffn_tile, xp_ref, op_ref, *weights))
    pl.when(step == n_prompt)(functools.partial(_ffn_tile, xs_ref, os_ref, *weights))


def _ffn(xp, xs, pre, post, wg, wu, wd, lead, convert):
    mp, d = xp.shape
    ms = xs.shape[0]
    d_ff = wg.shape[-1]
    n_prompt = mp // TOKEN_TILE
    n_chunks = d_ff // FFN_COL_CHUNK
    rows_p = _prompt_rows(TOKEN_TILE, d, n_prompt)
    c_in, c_out, c_shape = _convert_specs(convert, n_prompt)
    if lead is None:
        dma = [pltpu.SemaphoreType.DMA((3, n_chunks))]
    else:
        unit = FFN_COL_CHUNK // 2
        dma = [*_stage((d, unit)), *_stage((d, unit)), *_stage((unit, d))]
    out = pl.pallas_call(
        functools.partial(_ffn_kernel, n_prompt=n_prompt, lead=lead, n_convert=len(convert)),
        grid=(n_prompt + 1,),
        in_specs=[rows_p, _resident((ms, d)), _resident((1, d)), _resident((1, d)),
                  _hbm(), _hbm(), _hbm(), *c_in],
        out_specs=[rows_p, pl.BlockSpec((ms, d), lambda i: (0, 0)), *c_out],
        out_shape=[jax.ShapeDtypeStruct((mp, d), F32), jax.ShapeDtypeStruct((ms, d), F32),
                   *c_shape],
        scratch_shapes=[pltpu.VMEM((d, d_ff), BF16), pltpu.VMEM((d, d_ff), BF16),
                        pltpu.VMEM((d_ff, d), BF16), *dma],
        compiler_params=_PARAMS,
        name="ffn",
    )(xp, xs, pre.reshape(1, d), post.reshape(1, d), wg, wu, wd, *[w for w, _ in convert])
    return out[0], out[1], out[2:]


def _gla_tile(x_ref, y_ref, s_ref, pre_ref, post_ref, win_ref, wg2_ref, bg_ref, gn_ref, wout_ref,
              q_s, k_s, v_s, g_s, o_s, xn_s, qg_s, kg_s, kdt_s, el_s, a_s, u_s, *, nb, nc, L):
    heads = GLA_HEADS
    key = wg2_ref.shape[1]
    val = wout_ref.shape[0]
    dk, dv = key // heads, val // heads
    r = nb * nc * L
    blk = slice(0, r)

    xn = _rms(x_ref[...], pre_ref[...]).astype(BF16)
    xn_s[blk, :] = xn
    q_s[blk, :] = _dot(xn, win_ref[:, 0:key]) * (dk ** -0.5)
    k_s[blk, :] = _dot(xn, win_ref[:, key:2 * key])
    v_s[blk, :] = _dot(xn, win_ref[:, 2 * key:2 * key + val]).astype(BF16)
    zg = _dot(xn, win_ref[:, 2 * key + 2 * val:])
    z = _dot(zg.astype(BF16), wg2_ref[...].astype(BF16)) + bg_ref[...]
    g_s[blk, :] = jax.nn.log_sigmoid(z) / GLA_TAU

    row = lax.broadcasted_iota(jnp.int32, (L, L), 0)
    col = lax.broadcasted_iota(jnp.int32, (L, L), 1)
    causal = row >= col
    tri = jnp.where(causal, 1.0, 0.0).astype(BF16)

    chunks = range(nb * nc)
    rows = [slice(i * L, (i + 1) * L) for i in chunks]
    kslc = [slice(h * dk, (h + 1) * dk) for h in range(heads)]
    vslc = [slice(h * dv, (h + 1) * dv) for h in range(heads)]

    for i in chunks:
        gc = g_s[rows[i], :]
        g1 = gc.astype(BF16)
        rem = gc - g1.astype(F32)
        g2 = rem.astype(BF16)
        g3 = (rem - g2.astype(F32)).astype(BF16)
        g_s[rows[i], :] = _dot(tri, g1) + _dot(tri, g2) + _dot(tri, g3)

    for i in chunks:
        b = g_s[rows[i], :]
        b_last = b[L - 1:L, :]
        qc = q_s[rows[i], :]
        kc = k_s[rows[i], :]
        qg_s[rows[i], :] = (qc * jnp.exp(b)).astype(BF16)
        kg_s[rows[i], :] = (kc * jnp.exp(-b)).astype(BF16)
        kdt_s[i, :, 0:L] = jnp.transpose(kc * jnp.exp(b_last - b)).astype(BF16)
        el_s[i] = jnp.broadcast_to(jnp.exp(b_last), el_s.shape[1:])

    for i in chunks:
        for h in range(heads):
            a = lax.dot_general(qg_s[rows[i], kslc[h]], kg_s[rows[i], kslc[h]],
                                (((1,), (1,)), ((), ())), preferred_element_type=F32)
            a_s[i * heads + h, 0:L, 0:L] = jnp.where(causal, a, 0.0).astype(BF16)

    for i in chunks:
        for h in range(heads):
            lhs = jnp.concatenate([a_s[i * heads + h, 0:L, 0:L], kdt_s[i, kslc[h], 0:L]], axis=0)
            res = _dot(lhs, v_s[rows[i], vslc[h]])
            o_s[rows[i], vslc[h]] = res[:L]
            u_s[i * heads + h] = res[L:]

    for i in chunks:
        bi = i // nc
        for h in range(heads):
            s_h = s_ref[bi, h]
            o_s[rows[i], vslc[h]] += _dot(qg_s[rows[i], kslc[h]], s_h.astype(BF16))
            decay = jnp.transpose(jnp.broadcast_to(el_s[i, 0:1, kslc[h]], (dk, dk)))
            s_ref[bi, h] = s_h * jnp.tile(decay, (1, dv // dk)) + u_s[i * heads + h]

    gate = _silu(_dot(xn_s[blk, :], win_ref[:, 2 * key + val:2 * key + 2 * val]))
    gn = gn_ref[...]
    o_n = jnp.concatenate([_rms(o_s[blk, vslc[h]], gn) for h in range(heads)], axis=1)
    y = _dot((o_n * gate).astype(BF16), wout_ref[...])
    y_ref[...] = x_ref[...] + _rms(y, post_ref[...])


def _gla_kernel(*refs, n_prompt, tiles_per_seq, chunk_p, batch_s, chunk_s, n_convert):
    xp_ref, xs_ref, s0_ref = refs[:3]
    weights = refs[3:10]
    conv_src = refs[10:10 + n_convert]
    yp_ref, ys_ref, sp_ref, ss_ref = refs[10 + n_convert:14 + n_convert]
    conv_dst = refs[14 + n_convert:14 + 2 * n_convert]
    scratch = refs[14 + 2 * n_convert:]

    _convert(conv_src, conv_dst)

    def prompt():
        @pl.when(pl.program_id(0) % tiles_per_seq == 0)
        def _():
            sp_ref[...] = jnp.zeros(sp_ref.shape, F32)

        _gla_tile(xp_ref, yp_ref, sp_ref, *weights, *scratch,
                  nb=1, nc=xp_ref.shape[0] // chunk_p, L=chunk_p)

    def sample():
        ss_ref[...] = s0_ref[...]
        _gla_tile(xs_ref, ys_ref, ss_ref, *weights, *scratch,
                  nb=batch_s, nc=xs_ref.shape[0] // (batch_s * chunk_s), L=chunk_s)

    _two_streams(prompt, sample, n_prompt)


def _gla(xp, xs, s0, pre, post, w_in, w_gate2, b_gate, g_norm, w_out, convert,
         *, batch_p, batch_s):
    mp, d = xp.shape
    ms = xs.shape[0]
    heads = GLA_HEADS
    key = w_gate2.shape[1]
    val = w_out.shape[0]
    seq_p, seq_s = mp // batch_p, ms // batch_s
    chunk_p, chunk_s = min(seq_p, GLA_CHUNK), min(seq_s, GLA_CHUNK)
    tile = TOKEN_TILE
    assert seq_p % tile == 0 and tile % chunk_p == 0 and seq_s % chunk_s == 0 and ms <= tile
    tiles_per_seq = seq_p // tile
    n_prompt = mp // tile
    n_chunks = max(tile // chunk_p, ms // chunk_s)
    chunk = max(chunk_p, chunk_s)
    state_shape = (heads, key // heads, val // heads)
    rows_p = _prompt_rows(tile, d, n_prompt)
    state_p = pl.BlockSpec((1,) + state_shape,
                           lambda i: (jnp.minimum(i, n_prompt - 1) // tiles_per_seq, 0, 0, 0))
    state_s = pl.BlockSpec((batch_s,) + state_shape, lambda i: (0, 0, 0, 0))
    c_in, c_out, c_shape = _convert_specs(convert, n_prompt)
    kern = functools.partial(_gla_kernel, n_prompt=n_prompt, tiles_per_seq=tiles_per_seq,
                             chunk_p=chunk_p, batch_s=batch_s, chunk_s=chunk_s,
                             n_convert=len(convert))
    out = pl.pallas_call(
        kern,
        grid=(n_prompt + 1,),
        in_specs=[rows_p, _resident((ms, d)), _resident((batch_s,) + state_shape),
                  _resident((1, d)), _resident((1, d)), _resident(w_in.shape),
                  _resident(w_gate2.shape), _resident((1, key)), _resident((1, g_norm.shape[0])),
                  _resident(w_out.shape), *c_in],
        out_specs=[rows_p, pl.BlockSpec((ms, d), lambda i: (0, 0)), state_p, state_s, *c_out],
        out_shape=[jax.ShapeDtypeStruct((mp, d), F32), jax.ShapeDtypeStruct((ms, d), F32),
                   jax.ShapeDtypeStruct((batch_p,) + state_shape, F32),
                   jax.ShapeDtypeStruct((batch_s,) + state_shape, F32), *c_shape],
        scratch_shapes=[pltpu.VMEM((tile, key), F32), pltpu.VMEM((tile, key), F32),
                        pltpu.VMEM((tile, val), BF16), pltpu.VMEM((tile, key), F32),
                        pltpu.VMEM((tile, val), F32), pltpu.VMEM((tile, d), BF16),
                        pltpu.VMEM((tile, key), BF16), pltpu.VMEM((tile, key), BF16),
                        pltpu.VMEM((n_chunks, key, chunk), BF16),
                        pltpu.VMEM((n_chunks, 8, key), F32),
                        pltpu.VMEM((n_chunks * heads, chunk, chunk), BF16),
                        pltpu.VMEM((n_chunks * heads,) + state_shape[1:], F32)],
        compiler_params=_PARAMS,
        name="gla",
    )(xp, xs, s0, pre.reshape(1, d), post.reshape(1, d), w_in, w_gate2, b_gate.reshape(1, key),
      g_norm.reshape(1, -1), w_out, *[w for w, _ in convert])
    return out[0], out[1], out[2], out[3], out[4:]


def _sgu_tile(x_ref, y_ref, v_ref, pre_ref, post_ref, win_ref, bin_ref, lng_ref, lnb_ref,
              wmix_ref, bmix_ref, wout_ref, v_s, *, L):
    groups = SGU_GROUPS
    half = wout_ref.shape[0]
    gd = half // groups
    r = x_ref.shape[0]
    blk = slice(0, r)
    win = wmix_ref.shape[1]

    x = x_ref[...]
    xn = _rms(x, pre_ref[...]).astype(BF16)

    s1 = jnp.zeros((r, 1), F32)
    for g in range(groups):
        seg = slice(half + g * gd, half + (g + 1) * gd)
        vg = _gelu(_dot(xn, win_ref[:, seg]) + bin_ref[:, seg])
        v_s[blk, g * gd:(g + 1) * gd] = vg
        s1 = s1 + jnp.sum(vg, axis=-1, keepdims=True)
    mu = s1 / half

    def u_proj(g):
        seg = slice(g * gd, (g + 1) * gd)
        return _gelu(_dot(xn, win_ref[:, seg]) + bin_ref[:, seg])

    u_next = u_proj(0)
    s2 = jnp.zeros((r, 1), F32)
    for g in range(groups):
        xc = v_s[blk, g * gd:(g + 1) * gd] - mu
        s2 = s2 + jnp.sum(xc * xc, axis=-1, keepdims=True)
    rstd = lax.rsqrt(s2 / half + LN_EPS)

    row = lax.broadcasted_iota(jnp.int32, (win, win), 0)
    col = lax.broadcasted_iota(jnp.int32, (win, win), 1)
    mask = row >= col
    if L < win:
        mask = jnp.logical_and(mask, row // L == col // L)

    acc = jnp.zeros(x.shape, F32)
    for g in range(groups):
        seg = slice(g * gd, (g + 1) * gd)
        vn = (v_s[blk, seg] - mu) * rstd * lng_ref[:, seg] + lnb_ref[:, seg]
        if v_ref is not None:
            v_ref[:, seg] = vn
        vnb = vn.astype(BF16)
        wm = jnp.where(mask, wmix_ref[g], 0.0).astype(BF16)
        bcol = bmix_ref[:, g:g + 1]
        mixed = jnp.concatenate(
            [_dot(wm, vnb[c * win:(c + 1) * win, :]) + bcol for c in range(r // win)], axis=0)
        u = u_next
        if g + 1 < groups:
            u_next = u_proj(g + 1)
        acc = acc + _dot((u * mixed).astype(BF16), wout_ref[seg, :])
    y_ref[...] = x + _rms(acc, post_ref[...])


def _sgu_kernel(*refs, n_prompt, chunk_p, chunk_s, n_convert):
    (xp_ref, xs_ref, pre_ref, post_ref, win_ref, bin_ref, lng_ref, lnb_ref,
     wmix_p_ref, bmix_p_ref, wmix_s_ref, bmix_s_ref, wout_ref) = refs[:13]
    conv_src = refs[13:13 + n_convert]
    yp_ref, ys_ref, vs_ref = refs[13 + n_convert:16 + n_convert]
    conv_dst = refs[16 + n_convert:16 + 2 * n_convert]
    (v_s,) = refs[16 + 2 * n_convert:]
    shared = (pre_ref, post_ref, win_ref, bin_ref, lng_ref, lnb_ref)

    _convert(conv_src, conv_dst)
    _two_streams(
        functools.partial(_sgu_tile, xp_ref, yp_ref, None, *shared, wmix_p_ref, bmix_p_ref,
                          wout_ref, v_s, L=chunk_p),
        functools.partial(_sgu_tile, xs_ref, ys_ref, vs_ref, *shared, wmix_s_ref, bmix_s_ref,
                          wout_ref, v_s, L=chunk_s),
        n_prompt)


def _sgu(xp, xs, pre, post, w_in, b_in, ln_g, ln_b, w_s, b_s, w_out, convert, *, seq_p, seq_s):
    mp, d = xp.shape
    ms = xs.shape[0]
    half = w_out.shape[0]
    win = w_s.shape[1]
    chunk_p, chunk_s = min(seq_p, win), min(seq_s, win)
    tile = TOKEN_TILE
    assert chunk_p == win and tile % win == 0 and seq_p % win == 0
    assert ms % win == 0 and win % chunk_s == 0 and seq_s == chunk_s and ms <= tile
    n_prompt = mp // tile
    reps = win // chunk_s
    wmix_s = jnp.tile(w_s[:, :chunk_s, :chunk_s], (1, reps, reps))
    bmix_s = jnp.tile(b_s[:, :chunk_s].T, (reps, 1))
    rows_p = _prompt_rows(tile, d, n_prompt)
    c_in, c_out, c_shape = _convert_specs(convert, n_prompt)
    kern = functools.partial(_sgu_kernel, n_prompt=n_prompt, chunk_p=chunk_p, chunk_s=chunk_s,
                             n_convert=len(convert))
    out = pl.pallas_call(
        kern,
        grid=(n_prompt + 1,),
        in_specs=[rows_p, _resident((ms, d)), _resident((1, d)), _resident((1, d)),
                  _resident(w_in.shape), _resident((1, 2 * half)), _resident((1, half)),
                  _resident((1, half)), _resident(w_s.shape), _resident((win, w_s.shape[0])),
                  _resident(w_s.shape), _resident((win, w_s.shape[0])), _resident(w_out.shape),
                  *c_in],
        out_specs=[rows_p, pl.BlockSpec((ms, d), lambda i: (0, 0)),
                   pl.BlockSpec((ms, half), lambda i: (0, 0)), *c_out],
        out_shape=[jax.ShapeDtypeStruct((mp, d), F32), jax.ShapeDtypeStruct((ms, d), F32),
                   jax.ShapeDtypeStruct((ms, half), F32), *c_shape],
        scratch_shapes=[pltpu.VMEM((tile, half), F32)],
        compiler_params=_PARAMS,
        name="sgu",
    )(xp, xs, pre.reshape(1, d), post.reshape(1, d), w_in, b_in.reshape(1, -1),
      ln_g.reshape(1, -1), ln_b.reshape(1, -1), w_s, b_s.T, wmix_s, bmix_s, w_out,
      *[w for w, _ in convert])
    return out[0], out[1], out[2], out[3:]


def kernel(x_prompt, x_sample, state_gla, norm_pre, norm_post, ffn_w_gate, ffn_w_up, ffn_w_down,
           gla_w_in, gla_w_gate2, gla_b_gate, gla_norm, gla_w_out, sgu_w_in, sgu_b_in, sgu_ln_g,
           sgu_ln_b, sgu_w_s, sgu_b_s, sgu_w_out):
    batch, seq, d = x_prompt.shape
    dbatch, dseq, _ = x_sample.shape
    depth = norm_pre.shape[0]
    xp = x_prompt.reshape(batch * seq, d)
    xs = x_sample.reshape(dbatch * dseq, d)
    gla_p, gla_s, sgu_s = [], [], []

    def ffn_sources(i, k):
        return [(ffn_w_gate, (i, k)), (ffn_w_up, (i, k)), (ffn_w_down, (i, k))]

    sources = []
    for i in range(depth):
        j = i // 2
        mixer = [(gla_w_in, (j,)), (gla_w_out, (j,))] if i % 2 == 0 else \
            [(sgu_w_in, (j,)), (sgu_w_out, (j,))]
        sources += [ffn_sources(i, 0), mixer, ffn_sources(i, 1)]
    sources.append([])

    ready = None
    for i in range(depth):
        j = i // 2
        for part in range(3):
            nxt = sources[3 * i + part + 1]
            if part != 1:
                k, n = (0, 0) if part == 0 else (1, 2)
                if ready is None:
                    w, lead = (ffn_w_gate, ffn_w_up, ffn_w_down), (i, k)
                else:
                    w, lead = ready, None
                xp, xs, ready = _ffn(xp, xs, norm_pre[i, n], norm_post[i, n], *w, lead, nxt)
            elif i % 2 == 0:
                w_in, w_out = ready
                xp, xs, sp, ss, ready = _gla(
                    xp, xs, state_gla[j], norm_pre[i, 1], norm_post[i, 1], w_in, gla_w_gate2[j],
                    gla_b_gate[j], gla_norm[j], w_out, nxt, batch_p=batch, batch_s=dbatch)
                gla_p.append(sp)
                gla_s.append(ss)
            else:
                w_in, w_out = ready
                xp, xs, vs, ready = _sgu(
                    xp, xs, norm_pre[i, 1], norm_post[i, 1], w_in, sgu_b_in[j], sgu_ln_g[j],
                    sgu_ln_b[j], sgu_w_s[j], sgu_b_s[j], w_out, nxt, seq_p=seq, seq_s=dseq)
                sgu_s.append(vs.reshape(dbatch, dseq, -1))

    return (xp.reshape(batch, seq, d), xs.reshape(dbatch, dseq, d),
            jnp.stack(gla_p), jnp.stack(gla_s), jnp.stack(sgu_s))
```

```python
import functools

import jax
import jax.numpy as jnp
from jax import lax
from jax.experimental import pallas as pl
from jax.experimental.pallas import tpu as pltpu

F32 = jnp.float32
BF16 = jnp.bfloat16

NORM_EPS = 1e-6
LN_EPS = 1e-5
GLA_HEADS = 4
GLA_TAU = 16.0
GLA_CHUNK = 64
SGU_CHUNK = 128
SGU_GROUPS = 4

VMEM_LIMIT_BYTES = 56 * 1024 * 1024
TOKEN_TILE = 512
FFN_COL_CHUNK = 512


def _dot(a, b):
    return jnp.dot(a, b, preferred_element_type=F32)


def _rms(x, g, eps=NORM_EPS):
    return x * lax.rsqrt(jnp.mean(x * x, axis=-1, keepdims=True) + eps) * g


def _silu(x):
    return x * jax.nn.sigmoid(x)


def _gelu(x):
    return 0.5 * x * (1.0 + lax.erf(x * (0.5 ** 0.5)))


def _resident(shape):
    zeros = (0,) * len(shape)
    return pl.BlockSpec(shape, lambda *_: zeros, pipeline_mode=pl.Buffered(1))


def _hbm():
    return pl.BlockSpec(memory_space=pl.ANY)


def _prompt_rows(tile, width, n_prompt):
    return pl.BlockSpec((tile, width), lambda i: (jnp.minimum(i, n_prompt - 1), 0))


def _two_streams(body_prompt, body_sample, n_prompt):
    step = pl.program_id(0)
    pl.when(step < n_prompt)(body_prompt)
    pl.when(step == n_prompt)(body_sample)


_PARAMS = pltpu.CompilerParams(dimension_semantics=("arbitrary",),
                               vmem_limit_bytes=VMEM_LIMIT_BYTES)


def _convert_specs(sources, n_prompt):
    in_specs, out_specs, out_shape = [], [], []
    for w, lead in sources:
        rows, cols = w.shape[-2:]
        blk = rows // n_prompt
        assert blk * n_prompt == rows and blk % 16 == 0
        in_specs.append(pl.BlockSpec(
            (None,) * len(lead) + (blk, cols),
            lambda i, lead=lead: lead + (jnp.minimum(i, n_prompt - 1), 0)))
        out_specs.append(_prompt_rows(blk, cols, n_prompt))
        out_shape.append(jax.ShapeDtypeStruct((rows, cols), BF16))
    return in_specs, out_specs, out_shape


def _convert(src_refs, dst_refs):
    for src, dst in zip(src_refs, dst_refs):
        dst[...] = src[...].astype(BF16)


class _CastStream:
    def __init__(self, pieces, stage, sems):
        self.pieces, self.stage, self.sems = pieces, stage, sems

    def _copy(self, j):
        slot = j % 2
        return pltpu.make_async_copy(self.pieces[j][0], self.stage.at[slot], self.sems.at[slot])

    def start(self, j):
        if j < len(self.pieces):
            self._copy(j).start()

    def land(self, j):
        self._copy(j).wait()
        _, dst, idx = self.pieces[j]
        dst[idx] = self.stage[j % 2].astype(BF16)
        self.start(j + 2)


def _col_pieces(hbm, lead, dst, unit):
    n = dst.shape[1] // unit
    return [(hbm.at[lead + (slice(None), pl.ds(j * unit, unit))], dst,
             (slice(None), slice(j * unit, (j + 1) * unit))) for j in range(n)]


def _row_pieces(hbm, lead, dst, unit):
    n = dst.shape[0] // unit
    return [(hbm.at[lead + (pl.ds(j * unit, unit), slice(None))], dst,
             (slice(j * unit, (j + 1) * unit), slice(None))) for j in range(n)]


def _stage(shape):
    return [pltpu.VMEM((2,) + shape, F32), pltpu.SemaphoreType.DMA((2,))]


def _ffn_tile(x_ref, o_ref, pre_ref, post_ref, wg_ref, wu_ref, wd_ref, before_chunk=None,
              side_work=None):
    x = x_ref[...]
    xn = _rms(x, pre_ref[...]).astype(BF16)
    d_ff = wg_ref.shape[1]
    acc = jnp.zeros(x.shape, F32)
    for c in range(d_ff // FFN_COL_CHUNK):
        if before_chunk is not None:
            before_chunk(c)
        cols = slice(c * FFN_COL_CHUNK, (c + 1) * FFN_COL_CHUNK)
        h = _silu(_dot(xn, wg_ref[:, cols])) * _dot(xn, wu_ref[:, cols])
        acc = acc + _dot(h.astype(BF16), wd_ref[cols, :])
        if c == 0 and side_work is not None:
            side_work()
    o_ref[...] = x + 0.5 * _rms(acc, post_ref[...])


def _ffn_kernel(*refs, n_prompt, lead, n_convert):
    xp_ref, xs_ref, pre_ref, post_ref, wg_in, wu_in, wd_in = refs[:7]
    conv_src = refs[7:7 + n_convert]
    op_ref, os_ref = refs[7 + n_convert:9 + n_convert]
    conv_dst = refs[9 + n_convert:9 + 2 * n_convert]
    convert = functools.partial(_convert, conv_src, conv_dst)

    if lead is None:
        weights = (pre_ref, post_ref, wg_in, wu_in, wd_in)
        _two_streams(
            functools.partial(_ffn_tile, xp_ref, op_ref, *weights, side_work=convert),
            functools.partial(_ffn_tile, xs_ref, os_ref, *weights), n_prompt)
        return

    wg_ref, wu_ref, wd_ref, stage_g, sem_g, stage_u, sem_u, stage_d, sem_d = \
        refs[9 + 2 * n_convert:]
    weights = (pre_ref, post_ref, wg_ref, wu_ref, wd_ref)

    def first():
        unit = stage_d.shape[1]
        per_chunk = FFN_COL_CHUNK // unit
        streams = [_CastStream(_col_pieces(wg_in, lead, wg_ref, unit), stage_g, sem_g),
                   _CastStream(_col_pieces(wu_in, lead, wu_ref, unit), stage_u, sem_u),
                   _CastStream(_row_pieces(wd_in, lead, wd_ref, unit), stage_d, sem_d)]
        for s in streams:
            s.start(0)
            s.start(1)

        def land(c):
            for j in range(c * per_chunk, (c + 1) * per_chunk):
                for s in streams:
                    s.land(j)

        _ffn_tile(xp_ref, op_ref, *weights, before_chunk=land, side_work=convert)

    step = pl.program_id(0)
    pl.when(step == 0)(first)
    pl.when(jnp.logical_and(step > 0, step < n_prompt))(
        functools.partial(_ffn_tile, xp_ref, op_ref, *weights, side_work=convert))
    pl.when(step == n_prompt)(functools.partial(_ffn_tile, xs_ref, os_ref, *weights))


def _ffn(xp, xs, pre, post, wg, wu, wd, lead, convert):
    mp, d = xp.shape
    ms = xs.shape[0]
    d_ff = wg.shape[-1]
    n_prompt = mp // TOKEN_TILE
    rows_p = _prompt_rows(TOKEN_TILE, d, n_prompt)
    c_in, c_out, c_shape = _convert_specs(convert, n_prompt)
    if lead is None:
        w_specs = [_resident(wg.shape), _resident(wu.shape), _resident(wd.shape)]
        scratch = []
    else:
        unit = FFN_COL_CHUNK // 2
        w_specs = [_hbm(), _hbm(), _hbm()]
        scratch = [pltpu.VMEM((d, d_ff), BF16), pltpu.VMEM((d, d_ff), BF16),
                   pltpu.VMEM((d_ff, d), BF16), *_stage((d, unit)), *_stage((d, unit)),
                   *_stage((unit, d))]
    out = pl.pallas_call(
        functools.partial(_ffn_kernel, n_prompt=n_prompt, lead=lead, n_convert=len(convert)),
        grid=(n_prompt + 1,),
        in_specs=[rows_p, _resident((ms, d)), _resident((1, d)), _resident((1, d)),
                  *w_specs, *c_in],
        out_specs=[rows_p, pl.BlockSpec((ms, d), lambda i: (0, 0)), *c_out],
        out_shape=[jax.ShapeDtypeStruct((mp, d), F32), jax.ShapeDtypeStruct((ms, d), F32),
                   *c_shape],
        scratch_shapes=scratch,
        compiler_params=_PARAMS,
        name="ffn",
    )(xp, xs, pre.reshape(1, d), post.reshape(1, d), wg, wu, wd, *[w for w, _ in convert])
    return out[0], out[1], out[2:]


def _gla_tile(x_ref, y_ref, s_ref, pre_ref, post_ref, win_ref, wg2_ref, bg_ref, gn_ref, wout_ref,
              q_s, k_s, v_s, g_s, o_s, xn_s, qg_s, kg_s, kdt_s, el_s, a_s, u_s, *, nb, nc, L,
              side_work=None):
    heads = GLA_HEADS
    key = wg2_ref.shape[1]
    val = wout_ref.shape[0]
    dk, dv = key // heads, val // heads
    r = nb * nc * L
    blk = slice(0, r)

    xn = _rms(x_ref[...], pre_ref[...]).astype(BF16)
    xn_s[blk, :] = xn
    q_s[blk, :] = _dot(xn, win_ref[:, 0:key]) * (dk ** -0.5)
    k_s[blk, :] = _dot(xn, win_ref[:, key:2 * key])
    v_s[blk, :] = _dot(xn, win_ref[:, 2 * key:2 * key + val]).astype(BF16)
    zg = _dot(xn, win_ref[:, 2 * key + 2 * val:])
    z = _dot(zg.astype(BF16), wg2_ref[...].astype(BF16)) + bg_ref[...]
    g_s[blk, :] = jax.nn.log_sigmoid(z) / GLA_TAU
    if side_work is not None:
        side_work()

    row = lax.broadcasted_iota(jnp.int32, (L, L), 0)
    col = lax.broadcasted_iota(jnp.int32, (L, L), 1)
    causal = row >= col
    tri = jnp.where(causal, 1.0, 0.0).astype(BF16)

    chunks = range(nb * nc)
    rows = [slice(i * L, (i + 1) * L) for i in chunks]
    kslc = [slice(h * dk, (h + 1) * dk) for h in range(heads)]
    vslc = [slice(h * dv, (h + 1) * dv) for h in range(heads)]

    for i in chunks:
        gc = g_s[rows[i], :]
        g1 = gc.astype(BF16)
        rem = gc - g1.astype(F32)
        g2 = rem.astype(BF16)
        g3 = (rem - g2.astype(F32)).astype(BF16)
        g_s[rows[i], :] = _dot(tri, g1) + _dot(tri, g2) + _dot(tri, g3)

    for i in chunks:
        b = g_s[rows[i], :]
        b_last = b[L - 1:L, :]
        qc = q_s[rows[i], :]
        kc = k_s[rows[i], :]
        qg_s[rows[i], :] = (qc * jnp.exp(b)).astype(BF16)
        kg_s[rows[i], :] = (kc * jnp.exp(-b)).astype(BF16)
        kdt_s[i, :, 0:L] = jnp.transpose(kc * jnp.exp(b_last - b)).astype(BF16)
        el_s[i] = jnp.broadcast_to(jnp.exp(b_last), el_s.shape[1:])

    for i in chunks:
        for h in range(heads):
            a = lax.dot_general(qg_s[rows[i], kslc[h]], kg_s[rows[i], kslc[h]],
                                (((1,), (1,)), ((), ())), preferred_element_type=F32)
            a_s[i * heads + h, 0:L, 0:L] = jnp.where(causal, a, 0.0).astype(BF16)

    for i in chunks:
        for h in range(heads):
            lhs = jnp.concatenate([a_s[i * heads + h, 0:L, 0:L], kdt_s[i, kslc[h], 0:L]], axis=0)
            res = _dot(lhs, v_s[rows[i], vslc[h]])
            o_s[rows[i], vslc[h]] = res[:L]
            u_s[i * heads + h] = res[L:]

    for i in chunks:
        bi = i // nc
        for h in range(heads):
            s_h = s_ref[bi, h]
            o_s[rows[i], vslc[h]] += _dot(qg_s[rows[i], kslc[h]], s_h.astype(BF16))
            decay = jnp.transpose(jnp.broadcast_to(el_s[i, 0:1, kslc[h]], (dk, dk)))
            s_ref[bi, h] = s_h * jnp.tile(decay, (1, dv // dk)) + u_s[i * heads + h]

    gate = _silu(_dot(xn_s[blk, :], win_ref[:, 2 * key + val:2 * key + 2 * val]))
    gn = gn_ref[...]
    o_n = jnp.concatenate([_rms(o_s[blk, vslc[h]], gn) for h in range(heads)], axis=1)
    y = _dot((o_n * gate).astype(BF16), wout_ref[...])
    y_ref[...] = x_ref[...] + _rms(y, post_ref[...])


def _gla_kernel(*refs, n_prompt, tiles_per_seq, chunk_p, batch_s, chunk_s, n_convert):
    xp_ref, xs_ref, s0_ref = refs[:3]
    weights = refs[3:10]
    conv_src = refs[10:10 + n_convert]
    yp_ref, ys_ref, sp_ref, ss_ref = refs[10 + n_convert:14 + n_convert]
    conv_dst = refs[14 + n_convert:14 + 2 * n_convert]
    scratch = refs[14 + 2 * n_convert:]

    def prompt():
        @pl.when(pl.program_id(0) % tiles_per_seq == 0)
        def _():
            sp_ref[...] = jnp.zeros(sp_ref.shape, F32)

        _gla_tile(xp_ref, yp_ref, sp_ref, *weights, *scratch,
                  nb=1, nc=xp_ref.shape[0] // chunk_p, L=chunk_p,
                  side_work=functools.partial(_convert, conv_src, conv_dst))

    def sample():
        ss_ref[...] = s0_ref[...]
        _gla_tile(xs_ref, ys_ref, ss_ref, *weights, *scratch,
                  nb=batch_s, nc=xs_ref.shape[0] // (batch_s * chunk_s), L=chunk_s)

    _two_streams(prompt, sample, n_prompt)


def _gla(xp, xs, s0, pre, post, w_in, w_gate2, b_gate, g_norm, w_out, convert,
         *, batch_p, batch_s):
    mp, d = xp.shape
    ms = xs.shape[0]
    heads = GLA_HEADS
    key = w_gate2.shape[1]
    val = w_out.shape[0]
    seq_p, seq_s = mp // batch_p, ms // batch_s
    chunk_p, chunk_s = min(seq_p, GLA_CHUNK), min(seq_s, GLA_CHUNK)
    tile = TOKEN_TILE
    assert seq_p % tile == 0 and tile % chunk_p == 0 and seq_s % chunk_s == 0 and ms <= tile
    tiles_per_seq = seq_p // tile
    n_prompt = mp // tile
    n_chunks = max(tile // chunk_p, ms // chunk_s)
    chunk = max(chunk_p, chunk_s)
    state_shape = (heads, key // heads, val // heads)
    rows_p = _prompt_rows(tile, d, n_prompt)
    state_p = pl.BlockSpec((1,) + state_shape,
                           lambda i: (jnp.minimum(i, n_prompt - 1) // tiles_per_seq, 0, 0, 0))
    state_s = pl.BlockSpec((batch_s,) + state_shape, lambda i: (0, 0, 0, 0))
    c_in, c_out, c_shape = _convert_specs(convert, n_prompt)
    kern = functools.partial(_gla_kernel, n_prompt=n_prompt, tiles_per_seq=tiles_per_seq,
                             chunk_p=chunk_p, batch_s=batch_s, chunk_s=chunk_s,
                             n_convert=len(convert))
    out = pl.pallas_call(
        kern,
        grid=(n_prompt + 1,),
        in_specs=[rows_p, _resident((ms, d)), _resident((batch_s,) + state_shape),
                  _resident((1, d)), _resident((1, d)), _resident(w_in.shape),
                  _resident(w_gate2.shape), _resident((1, key)), _resident((1, g_norm.shape[0])),
                  _resident(w_out.shape), *c_in],
        out_specs=[rows_p, pl.BlockSpec((ms, d), lambda i: (0, 0)), state_p, state_s, *c_out],
        out_shape=[jax.ShapeDtypeStruct((mp, d), F32), jax.ShapeDtypeStruct((ms, d), F32),
                   jax.ShapeDtypeStruct((batch_p,) + state_shape, F32),
                   jax.ShapeDtypeStruct((batch_s,) + state_shape, F32), *c_shape],
        scratch_shapes=[pltpu.VMEM((tile, key), F32), pltpu.VMEM((tile, key), F32),
                        pltpu.VMEM((tile, val), BF16), pltpu.VMEM((tile, key), F32),
                        pltpu.VMEM((tile, val), F32), pltpu.VMEM((tile, d), BF16),
                        pltpu.VMEM((tile, key), BF16), pltpu.VMEM((tile, key), BF16),
                        pltpu.VMEM((n_chunks, key, chunk), BF16),
                        pltpu.VMEM((n_chunks, 8, key), F32),
                        pltpu.VMEM((n_chunks * heads, chunk, chunk), BF16),
                        pltpu.VMEM((n_chunks * heads,) + state_shape[1:], F32)],
        compiler_params=_PARAMS,
        name="gla",
    )(xp, xs, s0, pre.reshape(1, d), post.reshape(1, d), w_in, w_gate2, b_gate.reshape(1, key),
      g_norm.reshape(1, -1), w_out, *[w for w, _ in convert])
    return out[0], out[1], out[2], out[3], out[4:]


def _sgu_tile(x_ref, y_ref, v_ref, pre_ref, post_ref, win_ref, bin_ref, lng_ref, lnb_ref,
              wmix_ref, bmix_ref, wout_ref, v_s, *, L, side_work=None):
    groups = SGU_GROUPS
    half = wout_ref.shape[0]
    gd = half // groups
    r = x_ref.shape[0]
    blk = slice(0, r)
    win = wmix_ref.shape[1]

    x = x_ref[...]
    xn = _rms(x, pre_ref[...]).astype(BF16)

    s1 = jnp.zeros((r, 1), F32)
    for g in range(groups):
        seg = slice(half + g * gd, half + (g + 1) * gd)
        vg = _gelu(_dot(xn, win_ref[:, seg]) + bin_ref[:, seg])
        v_s[blk, g * gd:(g + 1) * gd] = vg
        s1 = s1 + jnp.sum(vg, axis=-1, keepdims=True)
    mu = s1 / half
    if side_work is not None:
        side_work()

    def u_proj(g):
        seg = slice(g * gd, (g + 1) * gd)
        return _gelu(_dot(xn, win_ref[:, seg]) + bin_ref[:, seg])

    u_next = u_proj(0)
    s2 = jnp.zeros((r, 1), F32)
    for g in range(groups):
        xc = v_s[blk, g * gd:(g + 1) * gd] - mu
        s2 = s2 + jnp.sum(xc * xc, axis=-1, keepdims=True)
    rstd = lax.rsqrt(s2 / half + LN_EPS)

    row = lax.broadcasted_iota(jnp.int32, (win, win), 0)
    col = lax.broadcasted_iota(jnp.int32, (win, win), 1)
    mask = row >= col
    if L < win:
        mask = jnp.logical_and(mask, row // L == col // L)

    acc = jnp.zeros(x.shape, F32)
    for g in range(groups):
        seg = slice(g * gd, (g + 1) * gd)
        vn = (v_s[blk, seg] - mu) * rstd * lng_ref[:, seg] + lnb_ref[:, seg]
        if v_ref is not None:
            v_ref[:, seg] = vn
        vnb = vn.astype(BF16)
        wm = jnp.where(mask, wmix_ref[g], 0.0).astype(BF16)
        bcol = bmix_ref[:, g:g + 1]
        mixed = jnp.concatenate(
            [_dot(wm, vnb[c * win:(c + 1) * win, :]) + bcol for c in range(r // win)], axis=0)
        u = u_next
        if g + 1 < groups:
            u_next = u_proj(g + 1)
        acc = acc + _dot((u * mixed).astype(BF16), wout_ref[seg, :])
    y_ref[...] = x + _rms(acc, post_ref[...])


def _sgu_kernel(*refs, n_prompt, chunk_p, chunk_s, n_convert):
    (xp_ref, xs_ref, pre_ref, post_ref, win_ref, bin_ref, lng_ref, lnb_ref,
     wmix_p_ref, bmix_p_ref, wmix_s_ref, bmix_s_ref, wout_ref) = refs[:13]
    conv_src = refs[13:13 + n_convert]
    yp_ref, ys_ref, vs_ref = refs[13 + n_convert:16 + n_convert]
    conv_dst = refs[16 + n_convert:16 + 2 * n_convert]
    (v_s,) = refs[16 + 2 * n_convert:]
    shared = (pre_ref, post_ref, win_ref, bin_ref, lng_ref, lnb_ref)

    _two_streams(
        functools.partial(_sgu_tile, xp_ref, yp_ref, None, *shared, wmix_p_ref, bmix_p_ref,
                          wout_ref, v_s, L=chunk_p,
                          side_work=functools.partial(_convert, conv_src, conv_dst)),
        functools.partial(_sgu_tile, xs_ref, ys_ref, vs_ref, *shared, wmix_s_ref, bmix_s_ref,
                          wout_ref, v_s, L=chunk_s),
        n_prompt)


def _sgu(xp, xs, pre, post, w_in, b_in, ln_g, ln_b, w_s, b_s, w_out, convert, *, seq_p, seq_s):
    mp, d = xp.shape
    ms = xs.shape[0]
    half = w_out.shape[0]
    win = w_s.shape[1]
    chunk_p, chunk_s = min(seq_p, win), min(seq_s, win)
    tile = TOKEN_TILE
    assert chunk_p == win and tile % win == 0 and seq_p % win == 0
    assert ms % win == 0 and win % chunk_s == 0 and seq_s == chunk_s and ms <= tile
    n_prompt = mp // tile
    reps = win // chunk_s
    wmix_s = jnp.tile(w_s[:, :chunk_s, :chunk_s], (1, reps, reps))
    bmix_s = jnp.tile(b_s[:, :chunk_s].T, (reps, 1))
    rows_p = _prompt_rows(tile, d, n_prompt)
    c_in, c_out, c_shape = _convert_specs(convert, n_prompt)
    kern = functools.partial(_sgu_kernel, n_prompt=n_prompt, chunk_p=chunk_p, chunk_s=chunk_s,
                             n_convert=len(convert))
    out = pl.pallas_call(
        kern,
        grid=(n_prompt + 1,),
        in_specs=[rows_p, _resident((ms, d)), _resident((1, d)), _resident((1, d)),
                  _resident(w_in.shape), _resident((1, 2 * half)), _resident((1, half)),
                  _resident((1, half)), _resident(w_s.shape), _resident((win, w_s.shape[0])),
                  _resident(w_s.shape), _resident((win, w_s.shape[0])), _resident(w_out.shape),
                  *c_in],
        out_specs=[rows_p, pl.BlockSpec((ms, d), lambda i: (0, 0)),
                   pl.BlockSpec((ms, half), lambda i: (0, 0)), *c_out],
        out_shape=[jax.ShapeDtypeStruct((mp, d), F32), jax.ShapeDtypeStruct((ms, d), F32),
                   jax.ShapeDtypeStruct((ms, half), F32), *c_shape],
        scratch_shapes=[pltpu.VMEM((tile, half), F32)],
        compiler_params=_PARAMS,
        name="sgu",
    )(xp, xs, pre.reshape(1, d), post.reshape(1, d), w_in, b_in.reshape(1, -1),
      ln_g.reshape(1, -1), ln_b.reshape(1, -1), w_s, b_s.T, wmix_s, bmix_s, w_out,
      *[w for w, _ in convert])
    return out[0], out[1], out[2], out[3:]


def kernel(x_prompt, x_sample, state_gla, norm_pre, norm_post, ffn_w_gate, ffn_w_up, ffn_w_down,
           gla_w_in, gla_w_gate2, gla_b_gate, gla_norm, gla_w_out, sgu_w_in, sgu_b_in, sgu_ln_g,
           sgu_ln_b, sgu_w_s, sgu_b_s, sgu_w_out):
    batch, seq, d = x_prompt.shape
    dbatch, dseq, _ = x_sample.shape
    depth = norm_pre.shape[0]
    xp = x_prompt.reshape(batch * seq, d)
    xs = x_sample.reshape(dbatch * dseq, d)
    gla_p, gla_s, sgu_s = [], [], []

    def ffn_sources(i, k):
        return [(ffn_w_gate, (i, k)), (ffn_w_up, (i, k)), (ffn_w_down, (i, k))]

    sources = []
    for i in range(depth):
        j = i // 2
        mixer = [(gla_w_in, (j,)), (gla_w_out, (j,))] if i % 2 == 0 else \
            [(sgu_w_in, (j,)), (sgu_w_out, (j,))]
        sources += [ffn_sources(i, 0), mixer, ffn_sources(i, 1)]
    sources.append([])

    ready = None
    for i in range(depth):
        j = i // 2
        for part in range(3):
            nxt = sources[3 * i + part + 1]
            if part != 1:
                k, n = (0, 0) if part == 0 else (1, 2)
                if ready is None:
                    w, lead = (ffn_w_gate, ffn_w_up, ffn_w_down), (i, k)
                else:
                    w, lead = ready, None
                xp, xs, ready = _ffn(xp, xs, norm_pre[i, n], norm_post[i, n], *w, lead, nxt)
            elif i % 2 == 0:
                w_in, w_out = ready
                xp, xs, sp, ss, ready = _gla(
                    xp, xs, state_gla[j], norm_pre[i, 1], norm_post[i, 1], w_in, gla_w_gate2[j],
                    gla_b_gate[j], gla_norm[j], w_out, nxt, batch_p=batch, batch_s=dbatch)
                gla_p.append(sp)
                gla_s.append(ss)
            else:
                w_in, w_out = ready
                xp, xs, vs, ready = _sgu(
                    xp, xs, norm_pre[i, 1], norm_post[i, 1], w_in, sgu_b_in[j], sgu_ln_g[j],
                    sgu_ln_b[j], sgu_w_s[j], sgu_b_s[j], w_out, nxt, seq_p=seq, seq_s=dseq)
                sgu_s.append(vs.reshape(dbatch, dseq, -1))

    return (xp.reshape(batch, seq, d), xs.reshape(dbatch, dseq, d),
            jnp.stack(gla_p), jnp.stack(gla_s), jnp.stack(sgu_s))
```

```python
import functools

import jax
import jax.numpy as jnp
from jax import lax
from jax.experimental import pallas as pl
from jax.experimental.pallas import tpu as pltpu

F32 = jnp.float32
BF16 = jnp.bfloat16

NORM_EPS = 1e-6
LN_EPS = 1e-5
GLA_HEADS = 4
GLA_TAU = 16.0
GLA_CHUNK = 64
SGU_CHUNK = 128
SGU_GROUPS = 4

VMEM_LIMIT_BYTES = 56 * 1024 * 1024
TOKEN_TILE = 512
FFN_COL_CHUNK = 512


def _dot(a, b):
    return jnp.dot(a, b, preferred_element_type=F32)


def _rms(x, g, eps=NORM_EPS):
    return x * lax.rsqrt(jnp.mean(x * x, axis=-1, keepdims=True) + eps) * g


def _silu(x):
    return x * jax.nn.sigmoid(x)


def _gelu(x):
    return 0.5 * x * (1.0 + lax.erf(x * (0.5 ** 0.5)))


def _resident(shape):
    zeros = (0,) * len(shape)
    return pl.BlockSpec(shape, lambda *_: zeros, pipeline_mode=pl.Buffered(1))


def _hbm():
    return pl.BlockSpec(memory_space=pl.ANY)


def _prompt_rows(tile, width, n_prompt):
    return pl.BlockSpec((tile, width), lambda i: (jnp.minimum(i, n_prompt - 1), 0))


def _two_streams(body_prompt, body_sample, n_prompt):
    step = pl.program_id(0)
    pl.when(step < n_prompt)(body_prompt)
    pl.when(step == n_prompt)(body_sample)


_PARAMS = pltpu.CompilerParams(dimension_semantics=("arbitrary",),
                               vmem_limit_bytes=VMEM_LIMIT_BYTES)


def _convert_specs(sources, n_prompt):
    in_specs, out_specs, out_shape = [], [], []
    for w, lead in sources:
        rows, cols = w.shape[-2:]
        blk = rows // n_prompt
        assert blk * n_prompt == rows and blk % 16 == 0
        in_specs.append(pl.BlockSpec(
            (None,) * len(lead) + (blk, cols),
            lambda i, lead=lead: lead + (jnp.minimum(i, n_prompt - 1), 0)))
        out_specs.append(_prompt_rows(blk, cols, n_prompt))
        out_shape.append(jax.ShapeDtypeStruct((rows, cols), BF16))
    return in_specs, out_specs, out_shape


def _convert(src_refs, dst_refs):
    for src, dst in zip(src_refs, dst_refs):
        dst[...] = src[...].astype(BF16)


class _CastStream:
    def __init__(self, pieces, stage, sems):
        self.pieces, self.stage, self.sems = pieces, stage, sems

    def _copy(self, j):
        slot = j % 2
        return pltpu.make_async_copy(self.pieces[j][0], self.stage.at[slot], self.sems.at[slot])

    def start(self, j):
        if j < len(self.pieces):
            self._copy(j).start()

    def land(self, j):
        self._copy(j).wait()
        _, dst, idx = self.pieces[j]
        dst[idx] = self.stage[j % 2].astype(BF16)
        self.start(j + 2)


def _col_pieces(hbm, lead, dst, unit):
    n = dst.shape[1] // unit
    return [(hbm.at[lead + (slice(None), pl.ds(j * unit, unit))], dst,
             (slice(None), slice(j * unit, (j + 1) * unit))) for j in range(n)]


def _row_pieces(hbm, lead, dst, unit):
    n = dst.shape[0] // unit
    return [(hbm.at[lead + (pl.ds(j * unit, unit), slice(None))], dst,
             (slice(j * unit, (j + 1) * unit), slice(None))) for j in range(n)]


def _stage(shape):
    return [pltpu.VMEM((2,) + shape, F32), pltpu.SemaphoreType.DMA((2,))]


def _ffn_tile(x_ref, o_ref, pre_ref, post_ref, wg_ref, wu_ref, wd_ref, before_chunk=None,
              side_work=None):
    x = x_ref[...]
    xn = _rms(x, pre_ref[...]).astype(BF16)
    d_ff = wg_ref.shape[1]
    acc = jnp.zeros(x.shape, F32)
    for c in range(d_ff // FFN_COL_CHUNK):
        if before_chunk is not None:
            before_chunk(c)
        cols = slice(c * FFN_COL_CHUNK, (c + 1) * FFN_COL_CHUNK)
        h = _silu(_dot(xn, wg_ref[:, cols])) * _dot(xn, wu_ref[:, cols])
        acc = acc + _dot(h.astype(BF16), wd_ref[cols, :])
        if c == 0 and side_work is not None:
            side_work()
    o_ref[...] = x + 0.5 * _rms(acc, post_ref[...])


def _ffn_kernel(*refs, n_prompt, lead, n_convert):
    xp_ref, xs_ref, pre_ref, post_ref, wg_in, wu_in, wd_in = refs[:7]
    conv_src = refs[7:7 + n_convert]
    op_ref, os_ref = refs[7 + n_convert:9 + n_convert]
    conv_dst = refs[9 + n_convert:9 + 2 * n_convert]
    convert = functools.partial(_convert, conv_src, conv_dst)

    if lead is None:
        weights = (pre_ref, post_ref, wg_in, wu_in, wd_in)
        _two_streams(
            functools.partial(_ffn_tile, xp_ref, op_ref, *weights, side_work=convert),
            functools.partial(_ffn_tile, xs_ref, os_ref, *weights), n_prompt)
        return

    wg_ref, wu_ref, wd_ref, stage_g, sem_g, stage_u, sem_u, stage_d, sem_d = \
        refs[9 + 2 * n_convert:]
    weights = (pre_ref, post_ref, wg_ref, wu_ref, wd_ref)

    def first():
        unit = stage_d.shape[1]
        per_chunk = FFN_COL_CHUNK // unit
        streams = [_CastStream(_col_pieces(wg_in, lead, wg_ref, unit), stage_g, sem_g),
                   _CastStream(_col_pieces(wu_in, lead, wu_ref, unit), stage_u, sem_u),
                   _CastStream(_row_pieces(wd_in, lead, wd_ref, unit), stage_d, sem_d)]
        for s in streams:
            s.start(0)
            s.start(1)

        def land(c):
            for j in range(c * per_chunk, (c + 1) * per_chunk):
                for s in streams:
                    s.land(j)

        _ffn_tile(xp_ref, op_ref, *weights, before_chunk=land, side_work=convert)

    step = pl.program_id(0)
    pl.when(step == 0)(first)
    pl.when(jnp.logical_and(step > 0, step < n_prompt))(
        functools.partial(_ffn_tile, xp_ref, op_ref, *weights, side_work=convert))
    pl.when(step == n_prompt)(functools.partial(_ffn_tile, xs_ref, os_ref, *weights))


def _ffn(xp, xs, pre, post, wg, wu, wd, lead, convert):
    mp, d = xp.shape
    ms = xs.shape[0]
    d_ff = wg.shape[-1]
    n_prompt = mp // TOKEN_TILE
    rows_p = _prompt_rows(TOKEN_TILE, d, n_prompt)
    c_in, c_out, c_shape = _convert_specs(convert, n_prompt)
    if lead is None:
        w_specs = [_resident(wg.shape), _resident(wu.shape), _resident(wd.shape)]
        scratch = []
    else:
        unit = FFN_COL_CHUNK // 2
        w_specs = [_hbm(), _hbm(), _hbm()]
        scratch = [pltpu.VMEM((d, d_ff), BF16), pltpu.VMEM((d, d_ff), BF16),
                   pltpu.VMEM((d_ff, d), BF16), *_stage((d, unit)), *_stage((d, unit)),
                   *_stage((unit, d))]
    out = pl.pallas_call(
        functools.partial(_ffn_kernel, n_prompt=n_prompt, lead=lead, n_convert=len(convert)),
        grid=(n_prompt + 1,),
        in_specs=[rows_p, _resident((ms, d)), _resident((1, d)), _resident((1, d)),
                  *w_specs, *c_in],
        out_specs=[rows_p, pl.BlockSpec((ms, d), lambda i: (0, 0)), *c_out],
        out_shape=[jax.ShapeDtypeStruct((mp, d), F32), jax.ShapeDtypeStruct((ms, d), F32),
                   *c_shape],
        scratch_shapes=scratch,
        compiler_params=_PARAMS,
        name="ffn",
    )(xp, xs, pre.reshape(1, d), post.reshape(1, d), wg, wu, wd, *[w for w, _ in convert])
    return out[0], out[1], out[2:]


def _gla_tile(x_ref, y_ref, s_ref, pre_ref, post_ref, win_ref, wg2_ref, bg_ref, gn_ref, wout_ref,
              q_s, k_s, v_s, g_s, o_s, xn_s, qg_s, kg_s, kdt_s, el_s, a_s, u_s, *, nb, nc, L,
              side_work=None):
    heads = GLA_HEADS
    key = wg2_ref.shape[1]
    val = wout_ref.shape[0]
    dk, dv = key // heads, val // heads
    r = nb * nc * L
    blk = slice(0, r)

    xn = _rms(x_ref[...], pre_ref[...]).astype(BF16)
    xn_s[blk, :] = xn
    zg = _dot(xn, win_ref[:, 2 * key + 2 * val:])
    q_s[blk, :] = _dot(xn, win_ref[:, 0:key]) * (dk ** -0.5)
    z = _dot(zg.astype(BF16), wg2_ref[...].astype(BF16)) + bg_ref[...]
    k_s[blk, :] = _dot(xn, win_ref[:, key:2 * key])
    g_s[blk, :] = (jnp.minimum(z, 0.0) - jnp.log1p(jnp.exp(-jnp.abs(z)))) / GLA_TAU
    if side_work is not None:
        side_work()
    v_s[blk, :] = _dot(xn, win_ref[:, 2 * key:2 * key + val]).astype(BF16)

    row = lax.broadcasted_iota(jnp.int32, (L, L), 0)
    col = lax.broadcasted_iota(jnp.int32, (L, L), 1)
    causal = row >= col
    tri = jnp.where(causal, 1.0, 0.0).astype(BF16)

    chunks = range(nb * nc)
    rows = [slice(i * L, (i + 1) * L) for i in chunks]
    kslc = [slice(h * dk, (h + 1) * dk) for h in range(heads)]
    vslc = [slice(h * dv, (h + 1) * dv) for h in range(heads)]

    for i in chunks:
        gc = g_s[rows[i], :]
        g1 = gc.astype(BF16)
        rem = gc - g1.astype(F32)
        g2 = rem.astype(BF16)
        g3 = (rem - g2.astype(F32)).astype(BF16)
        g_s[rows[i], :] = _dot(tri, g1) + _dot(tri, g2) + _dot(tri, g3)

    gate = _silu(_dot(xn_s[blk, :], win_ref[:, 2 * key + val:2 * key + 2 * val]))

    for i in chunks:
        b = g_s[rows[i], :]
        b_last = b[L - 1:L, :]
        qc = q_s[rows[i], :]
        kc = k_s[rows[i], :]
        qg_s[rows[i], :] = (qc * jnp.exp(b)).astype(BF16)
        kg_s[rows[i], :] = (kc * jnp.exp(-b)).astype(BF16)
        kdt_s[i, :, 0:L] = jnp.transpose(kc * jnp.exp(b_last - b)).astype(BF16)
        el_s[i] = jnp.broadcast_to(jnp.exp(b_last), el_s.shape[1:])

    for i in chunks:
        for h in range(heads):
            a = lax.dot_general(qg_s[rows[i], kslc[h]], kg_s[rows[i], kslc[h]],
                                (((1,), (1,)), ((), ())), preferred_element_type=F32)
            a_s[i * heads + h, 0:L, 0:L] = jnp.where(causal, a, 0.0).astype(BF16)

    for i in chunks:
        for h in range(heads):
            lhs = jnp.concatenate([a_s[i * heads + h, 0:L, 0:L], kdt_s[i, kslc[h], 0:L]], axis=0)
            res = _dot(lhs, v_s[rows[i], vslc[h]])
            o_s[rows[i], vslc[h]] = res[:L]
            u_s[i * heads + h] = res[L:]

    for i in chunks:
        bi = i // nc
        for h in range(heads):
            s_h = s_ref[bi, h]
            o_s[rows[i], vslc[h]] += _dot(qg_s[rows[i], kslc[h]], s_h.astype(BF16))
            decay = jnp.transpose(jnp.broadcast_to(el_s[i, 0:1, kslc[h]], (dk, dk)))
            s_ref[bi, h] = s_h * jnp.tile(decay, (1, dv // dk)) + u_s[i * heads + h]

    gn = gn_ref[...]
    o_n = jnp.concatenate([_rms(o_s[blk, vslc[h]], gn) for h in range(heads)], axis=1)
    y = _dot((o_n * gate).astype(BF16), wout_ref[...])
    y_ref[...] = x_ref[...] + _rms(y, post_ref[...])


def _gla_kernel(*refs, n_prompt, tiles_per_seq, chunk_p, batch_s, chunk_s, n_convert):
    xp_ref, xs_ref, s0_ref = refs[:3]
    weights = refs[3:10]
    conv_src = refs[10:10 + n_convert]
    yp_ref, ys_ref, sp_ref, ss_ref = refs[10 + n_convert:14 + n_convert]
    conv_dst = refs[14 + n_convert:14 + 2 * n_convert]
    scratch = refs[14 + 2 * n_convert:]

    def prompt():
        @pl.when(pl.program_id(0) % tiles_per_seq == 0)
        def _():
            sp_ref[...] = jnp.zeros(sp_ref.shape, F32)

        _gla_tile(xp_ref, yp_ref, sp_ref, *weights, *scratch,
                  nb=1, nc=xp_ref.shape[0] // chunk_p, L=chunk_p,
                  side_work=functools.partial(_convert, conv_src, conv_dst))

    def sample():
        ss_ref[...] = s0_ref[...]
        _gla_tile(xs_ref, ys_ref, ss_ref, *weights, *scratch,
                  nb=batch_s, nc=xs_ref.shape[0] // (batch_s * chunk_s), L=chunk_s)

    _two_streams(prompt, sample, n_prompt)


def _gla(xp, xs, s0, pre, post, w_in, w_gate2, b_gate, g_norm, w_out, convert,
         *, batch_p, batch_s):
    mp, d = xp.shape
    ms = xs.shape[0]
    heads = GLA_HEADS
    key = w_gate2.shape[1]
    val = w_out.shape[0]
    seq_p, seq_s = mp // batch_p, ms // batch_s
    chunk_p, chunk_s = min(seq_p, GLA_CHUNK), min(seq_s, GLA_CHUNK)
    tile = TOKEN_TILE
    assert seq_p % tile == 0 and tile % chunk_p == 0 and seq_s % chunk_s == 0 and ms <= tile
    tiles_per_seq = seq_p // tile
    n_prompt = mp // tile
    n_chunks = max(tile // chunk_p, ms // chunk_s)
    chunk = max(chunk_p, chunk_s)
    state_shape = (heads, key // heads, val // heads)
    rows_p = _prompt_rows(tile, d, n_prompt)
    state_p = pl.BlockSpec((1,) + state_shape,
                           lambda i: (jnp.minimum(i, n_prompt - 1) // tiles_per_seq, 0, 0, 0))
    state_s = pl.BlockSpec((batch_s,) + state_shape, lambda i: (0, 0, 0, 0))
    c_in, c_out, c_shape = _convert_specs(convert, n_prompt)
    kern = functools.partial(_gla_kernel, n_prompt=n_prompt, tiles_per_seq=tiles_per_seq,
                             chunk_p=chunk_p, batch_s=batch_s, chunk_s=chunk_s,
                             n_convert=len(convert))
    out = pl.pallas_call(
        kern,
        grid=(n_prompt + 1,),
        in_specs=[rows_p, _resident((ms, d)), _resident((batch_s,) + state_shape),
                  _resident((1, d)), _resident((1, d)), _resident(w_in.shape),
                  _resident(w_gate2.shape), _resident((1, key)), _resident((1, g_norm.shape[0])),
                  _resident(w_out.shape), *c_in],
        out_specs=[rows_p, pl.BlockSpec((ms, d), lambda i: (0, 0)), state_p, state_s, *c_out],
        out_shape=[jax.ShapeDtypeStruct((mp, d), F32), jax.ShapeDtypeStruct((ms, d), F32),
                   jax.ShapeDtypeStruct((batch_p,) + state_shape, F32),
                   jax.ShapeDtypeStruct((batch_s,) + state_shape, F32), *c_shape],
        scratch_shapes=[pltpu.VMEM((tile, key), F32), pltpu.VMEM((tile, key), F32),
                        pltpu.VMEM((tile, val), BF16), pltpu.VMEM((tile, key), F32),
                        pltpu.VMEM((tile, val), F32), pltpu.VMEM((tile, d), BF16),
                        pltpu.VMEM((tile, key), BF16), pltpu.VMEM((tile, key), BF16),
                        pltpu.VMEM((n_chunks, key, chunk), BF16),
                        pltpu.VMEM((n_chunks, 8, key), F32),
                        pltpu.VMEM((n_chunks * heads, chunk, chunk), BF16),
                        pltpu.VMEM((n_chunks * heads,) + state_shape[1:], F32)],
        compiler_params=_PARAMS,
        name="gla",
    )(xp, xs, s0, pre.reshape(1, d), post.reshape(1, d), w_in, w_gate2, b_gate.reshape(1, key),
      g_norm.reshape(1, -1), w_out, *[w for w, _ in convert])
    return out[0], out[1], out[2], out[3], out[4:]


def _sgu_tile(x_ref, y_ref, v_ref, pre_ref, post_ref, win_ref, bin_ref, lng_ref, lnb_ref,
              wmix_ref, bmix_ref, wout_ref, v_s, *, L, side_work=None):
    groups = SGU_GROUPS
    half = wout_ref.shape[0]
    gd = half // groups
    r = x_ref.shape[0]
    blk = slice(0, r)
    win = wmix_ref.shape[1]

    x = x_ref[...]
    xn = _rms(x, pre_ref[...]).astype(BF16)

    s1 = jnp.zeros((r, 1), F32)
    for g in range(groups):
        seg = slice(half + g * gd, half + (g + 1) * gd)
        vg = _gelu(_dot(xn, win_ref[:, seg]) + bin_ref[:, seg])
        v_s[blk, g * gd:(g + 1) * gd] = vg
        s1 = s1 + jnp.sum(vg, axis=-1, keepdims=True)
    mu = s1 / half
    if side_work is not None:
        side_work()

    def u_proj(g):
        seg = slice(g * gd, (g + 1) * gd)
        return _gelu(_dot(xn, win_ref[:, seg]) + bin_ref[:, seg])

    u_ready = [u_proj(g) for g in range(min(2, groups))]
    s2 = jnp.zeros((r, 1), F32)
    for g in range(groups):
        xc = v_s[blk, g * gd:(g + 1) * gd] - mu
        s2 = s2 + jnp.sum(xc * xc, axis=-1, keepdims=True)
    rstd = lax.rsqrt(s2 / half + LN_EPS)

    row = lax.broadcasted_iota(jnp.int32, (win, win), 0)
    col = lax.broadcasted_iota(jnp.int32, (win, win), 1)
    mask = row >= col
    if L < win:
        mask = jnp.logical_and(mask, row // L == col // L)

    d_half = x.shape[1] // 2
    acc = [jnp.zeros((r, d_half), F32), jnp.zeros((r, d_half), F32)]
    for g in range(groups):
        seg = slice(g * gd, (g + 1) * gd)
        if g + 2 < groups:
            u_ready.append(u_proj(g + 2))
        vn = (v_s[blk, seg] - mu) * rstd * lng_ref[:, seg] + lnb_ref[:, seg]
        if v_ref is not None:
            v_ref[:, seg] = vn
        vnb = vn.astype(BF16)
        wm = jnp.where(mask, wmix_ref[g], 0.0).astype(BF16)
        bcol = bmix_ref[:, g:g + 1]
        mixed = jnp.concatenate(
            [_dot(wm, vnb[c * win:(c + 1) * win, :]) + bcol for c in range(r // win)], axis=0)
        p = (u_ready[g] * mixed).astype(BF16)
        acc = [acc[0] + _dot(p, wout_ref[seg, 0:d_half]),
               acc[1] + _dot(p, wout_ref[seg, d_half:])]
    y_ref[...] = x + _rms(jnp.concatenate(acc, axis=1), post_ref[...])


def _sgu_kernel(*refs, n_prompt, chunk_p, chunk_s, n_convert):
    (xp_ref, xs_ref, pre_ref, post_ref, win_ref, bin_ref, lng_ref, lnb_ref,
     wmix_p_ref, bmix_p_ref, wmix_s_ref, bmix_s_ref, wout_ref) = refs[:13]
    conv_src = refs[13:13 + n_convert]
    yp_ref, ys_ref, vs_ref = refs[13 + n_convert:16 + n_convert]
    conv_dst = refs[16 + n_convert:16 + 2 * n_convert]
    (v_s,) = refs[16 + 2 * n_convert:]
    shared = (pre_ref, post_ref, win_ref, bin_ref, lng_ref, lnb_ref)

    _two_streams(
        functools.partial(_sgu_tile, xp_ref, yp_ref, None, *shared, wmix_p_ref, bmix_p_ref,
                          wout_ref, v_s, L=chunk_p,
                          side_work=functools.partial(_convert, conv_src, conv_dst)),
        functools.partial(_sgu_tile, xs_ref, ys_ref, vs_ref, *shared, wmix_s_ref, bmix_s_ref,
                          wout_ref, v_s, L=chunk_s),
        n_prompt)


def _sgu(xp, xs, pre, post, w_in, b_in, ln_g, ln_b, w_s, b_s, w_out, convert, *, seq_p, seq_s):
    mp, d = xp.shape
    ms = xs.shape[0]
    half = w_out.shape[0]
    win = w_s.shape[1]
    chunk_p, chunk_s = min(seq_p, win), min(seq_s, win)
    tile = TOKEN_TILE
    assert chunk_p == win and tile % win == 0 and seq_p % win == 0
    assert ms % win == 0 and win % chunk_s == 0 and seq_s == chunk_s and ms <= tile
    n_prompt = mp // tile
    reps = win // chunk_s
    wmix_s = jnp.tile(w_s[:, :chunk_s, :chunk_s], (1, reps, reps))
    bmix_s = jnp.tile(b_s[:, :chunk_s].T, (reps, 1))
    rows_p = _prompt_rows(tile, d, n_prompt)
    c_in, c_out, c_shape = _convert_specs(convert, n_prompt)
    kern = functools.partial(_sgu_kernel, n_prompt=n_prompt, chunk_p=chunk_p, chunk_s=chunk_s,
                             n_convert=len(convert))
    out = pl.pallas_call(
        kern,
        grid=(n_prompt + 1,),
        in_specs=[rows_p, _resident((ms, d)), _resident((1, d)), _resident((1, d)),
                  _resident(w_in.shape), _resident((1, 2 * half)), _resident((1, half)),
                  _resident((1, half)), _resident(w_s.shape), _resident((win, w_s.shape[0])),
                  _resident(w_s.shape), _resident((win, w_s.shape[0])), _resident(w_out.shape),
                  *c_in],
        out_specs=[rows_p, pl.BlockSpec((ms, d), lambda i: (0, 0)),
                   pl.BlockSpec((ms, half), lambda i: (0, 0)), *c_out],
        out_shape=[jax.ShapeDtypeStruct((mp, d), F32), jax.ShapeDtypeStruct((ms, d), F32),
                   jax.ShapeDtypeStruct((ms, half), F32), *c_shape],
        scratch_shapes=[pltpu.VMEM((tile, half), F32)],
        compiler_params=_PARAMS,
        name="sgu",
    )(xp, xs, pre.reshape(1, d), post.reshape(1, d), w_in, b_in.reshape(1, -1),
      ln_g.reshape(1, -1), ln_b.reshape(1, -1), w_s, b_s.T, wmix_s, bmix_s, w_out,
      *[w for w, _ in convert])
    return out[0], out[1], out[2], out[3:]


def kernel(x_prompt, x_sample, state_gla, norm_pre, norm_post, ffn_w_gate, ffn_w_up, ffn_w_down,
           gla_w_in, gla_w_gate2, gla_b_gate, gla_norm, gla_w_out, sgu_w_in, sgu_b_in, sgu_ln_g,
           sgu_ln_b, sgu_w_s, sgu_b_s, sgu_w_out):
    batch, seq, d = x_prompt.shape
    dbatch, dseq, _ = x_sample.shape
    depth = norm_pre.shape[0]
    xp = x_prompt.reshape(batch * seq, d)
    xs = x_sample.reshape(dbatch * dseq, d)
    gla_p, gla_s, sgu_s = [], [], []

    def ffn_sources(i, k):
        return [(ffn_w_gate, (i, k)), (ffn_w_up, (i, k)), (ffn_w_down, (i, k))]

    sources = []
    for i in range(depth):
        j = i // 2
        mixer = [(gla_w_in, (j,)), (gla_w_out, (j,))] if i % 2 == 0 else \
            [(sgu_w_in, (j,)), (sgu_w_out, (j,))]
        sources += [ffn_sources(i, 0), mixer, ffn_sources(i, 1)]
    sources.append([])

    ready = None
    for i in range(depth):
        j = i // 2
        for part in range(3):
            nxt = sources[3 * i + part + 1]
            if part != 1:
                k, n = (0, 0) if part == 0 else (1, 2)
                if ready is None:
                    w, lead = (ffn_w_gate, ffn_w_up, ffn_w_down), (i, k)
                else:
                    w, lead = ready, None
                xp, xs, ready = _ffn(xp, xs, norm_pre[i, n], norm_post[i, n], *w, lead, nxt)
            elif i % 2 == 0:
                w_in, w_out = ready
                xp, xs, sp, ss, ready = _gla(
                    xp, xs, state_gla[j], norm_pre[i, 1], norm_post[i, 1], w_in, gla_w_gate2[j],
                    gla_b_gate[j], gla_norm[j], w_out, nxt, batch_p=batch, batch_s=dbatch)
                gla_p.append(sp)
                gla_s.append(ss)
            else:
                w_in, w_out = ready
                xp, xs, vs, ready = _sgu(
                    xp, xs, norm_pre[i, 1], norm_post[i, 1], w_in, sgu_b_in[j], sgu_ln_g[j],
                    sgu_ln_b[j], sgu_w_s[j], sgu_b_s[j], w_out, nxt, seq_p=seq, seq_s=dseq)
                sgu_s.append(vs.reshape(dbatch, dseq, -1))

    return (xp.reshape(batch, seq, d), xs.reshape(dbatch, dseq, d),
            jnp.stack(gla_p), jnp.stack(gla_s), jnp.stack(sgu_s))
```

```python
import functools

import jax
import jax.numpy as jnp
from jax import lax
from jax.experimental import pallas as pl
from jax.experimental.pallas import tpu as pltpu

F32 = jnp.float32
BF16 = jnp.bfloat16

NORM_EPS = 1e-6
LN_EPS = 1e-5
GLA_HEADS = 4
GLA_TAU = 16.0
GLA_CHUNK = 64
SGU_CHUNK = 128
SGU_GROUPS = 4

VMEM_LIMIT_BYTES = 56 * 1024 * 1024
TOKEN_TILE = 512
FFN_COL_CHUNK = 512


def _dot(a, b):
    return jnp.dot(a, b, preferred_element_type=F32)


def _rms(x, g, eps=NORM_EPS):
    return x * lax.rsqrt(jnp.mean(x * x, axis=-1, keepdims=True) + eps) * g


def _silu(x):
    return x * jax.nn.sigmoid(x)


def _gelu(x):
    return 0.5 * x * (1.0 + lax.erf(x * (0.5 ** 0.5)))


def _resident(shape):
    zeros = (0,) * len(shape)
    return pl.BlockSpec(shape, lambda *_: zeros, pipeline_mode=pl.Buffered(1))


def _hbm():
    return pl.BlockSpec(memory_space=pl.ANY)


def _prompt_rows(tile, width, n_prompt):
    return pl.BlockSpec((tile, width), lambda i: (jnp.minimum(i, n_prompt - 1), 0))


def _two_streams(body_prompt, body_sample, n_prompt):
    step = pl.program_id(0)
    pl.when(step < n_prompt)(body_prompt)
    pl.when(step == n_prompt)(body_sample)


_PARAMS = pltpu.CompilerParams(dimension_semantics=("arbitrary",),
                               vmem_limit_bytes=VMEM_LIMIT_BYTES)


def _convert_specs(sources, n_prompt):
    in_specs, out_specs, out_shape = [], [], []
    for w, lead in sources:
        rows, cols = w.shape[-2:]
        blk = rows // n_prompt
        assert blk * n_prompt == rows and blk % 16 == 0
        in_specs.append(pl.BlockSpec(
            (None,) * len(lead) + (blk, cols),
            lambda i, lead=lead: lead + (jnp.minimum(i, n_prompt - 1), 0)))
        out_specs.append(_prompt_rows(blk, cols, n_prompt))
        out_shape.append(jax.ShapeDtypeStruct((rows, cols), BF16))
    return in_specs, out_specs, out_shape


def _convert(src_refs, dst_refs):
    for src, dst in zip(src_refs, dst_refs):
        dst[...] = src[...].astype(BF16)


class _CastStream:
    def __init__(self, pieces, stage, sems):
        self.pieces, self.stage, self.sems = pieces, stage, sems

    def _copy(self, j):
        slot = j % 2
        return pltpu.make_async_copy(self.pieces[j][0], self.stage.at[slot], self.sems.at[slot])

    def start(self, j):
        if j < len(self.pieces):
            self._copy(j).start()

    def land(self, j):
        self._copy(j).wait()
        _, dst, idx = self.pieces[j]
        dst[idx] = self.stage[j % 2].astype(BF16)
        self.start(j + 2)


def _col_pieces(hbm, lead, dst, unit):
    n = dst.shape[1] // unit
    return [(hbm.at[lead + (slice(None), pl.ds(j * unit, unit))], dst,
             (slice(None), slice(j * unit, (j + 1) * unit))) for j in range(n)]


def _row_pieces(hbm, lead, dst, unit):
    n = dst.shape[0] // unit
    return [(hbm.at[lead + (pl.ds(j * unit, unit), slice(None))], dst,
             (slice(j * unit, (j + 1) * unit), slice(None))) for j in range(n)]


def _stage(shape):
    return [pltpu.VMEM((2,) + shape, F32), pltpu.SemaphoreType.DMA((2,))]


def _ffn_tile(x_ref, o_ref, pre_ref, post_ref, wg_ref, wu_ref, wd_ref, before_chunk=None,
              side_work=None):
    x = x_ref[...]
    xn = _rms(x, pre_ref[...]).astype(BF16)
    d_ff = wg_ref.shape[1]
    acc = jnp.zeros(x.shape, F32)
    for c in range(d_ff // FFN_COL_CHUNK):
        if before_chunk is not None:
            before_chunk(c)
        cols = slice(c * FFN_COL_CHUNK, (c + 1) * FFN_COL_CHUNK)
        h = _silu(_dot(xn, wg_ref[:, cols])) * _dot(xn, wu_ref[:, cols])
        acc = acc + _dot(h.astype(BF16), wd_ref[cols, :])
        if c == 0 and side_work is not None:
            side_work()
    o_ref[...] = x + _rms(acc, 0.5 * post_ref[...])


def _ffn_kernel(*refs, n_prompt, lead, n_convert):
    xp_ref, xs_ref, pre_ref, post_ref, wg_in, wu_in, wd_in = refs[:7]
    conv_src = refs[7:7 + n_convert]
    op_ref, os_ref = refs[7 + n_convert:9 + n_convert]
    conv_dst = refs[9 + n_convert:9 + 2 * n_convert]
    convert = functools.partial(_convert, conv_src, conv_dst)

    if lead is None:
        weights = (pre_ref, post_ref, wg_in, wu_in, wd_in)
        _two_streams(
            functools.partial(_ffn_tile, xp_ref, op_ref, *weights, side_work=convert),
            functools.partial(_ffn_tile, xs_ref, os_ref, *weights), n_prompt)
        return

    wg_ref, wu_ref, wd_ref, stage_g, sem_g, stage_u, sem_u, stage_d, sem_d = \
        refs[9 + 2 * n_convert:]
    weights = (pre_ref, post_ref, wg_ref, wu_ref, wd_ref)

    def first():
        unit = stage_d.shape[1]
        per_chunk = FFN_COL_CHUNK // unit
        streams = [_CastStream(_col_pieces(wg_in, lead, wg_ref, unit), stage_g, sem_g),
                   _CastStream(_col_pieces(wu_in, lead, wu_ref, unit), stage_u, sem_u),
                   _CastStream(_row_pieces(wd_in, lead, wd_ref, unit), stage_d, sem_d)]
        for s in streams:
            s.start(0)
            s.start(1)

        def land(c):
            for j in range(c * per_chunk, (c + 1) * per_chunk):
                for s in streams:
                    s.land(j)

        _ffn_tile(xp_ref, op_ref, *weights, before_chunk=land, side_work=convert)

    step = pl.program_id(0)
    pl.when(step == 0)(first)
    pl.when(jnp.logical_and(step > 0, step < n_prompt))(
        functools.partial(_ffn_tile, xp_ref, op_ref, *weights, side_work=convert))
    pl.when(step == n_prompt)(functools.partial(_ffn_tile, xs_ref, os_ref, *weights))


def _ffn(xp, xs, pre, post, wg, wu, wd, lead, convert):
    mp, d = xp.shape
    ms = xs.shape[0]
    d_ff = wg.shape[-1]
    n_prompt = mp // TOKEN_TILE
    rows_p = _prompt_rows(TOKEN_TILE, d, n_prompt)
    c_in, c_out, c_shape = _convert_specs(convert, n_prompt)
    if lead is None:
        w_specs = [_resident(wg.shape), _resident(wu.shape), _resident(wd.shape)]
        scratch = []
    else:
        unit = FFN_COL_CHUNK // 2
        w_specs = [_hbm(), _hbm(), _hbm()]
        scratch = [pltpu.VMEM((d, d_ff), BF16), pltpu.VMEM((d, d_ff), BF16),
                   pltpu.VMEM((d_ff, d), BF16), *_stage((d, unit)), *_stage((d, unit)),
                   *_stage((unit, d))]
    out = pl.pallas_call(
        functools.partial(_ffn_kernel, n_prompt=n_prompt, lead=lead, n_convert=len(convert)),
        grid=(n_prompt + 1,),
        in_specs=[rows_p, _resident((ms, d)), _resident((1, d)), _resident((1, d)),
                  *w_specs, *c_in],
        out_specs=[rows_p, pl.BlockSpec((ms, d), lambda i: (0, 0)), *c_out],
        out_shape=[jax.ShapeDtypeStruct((mp, d), F32), jax.ShapeDtypeStruct((ms, d), F32),
                   *c_shape],
        scratch_shapes=scratch,
        compiler_params=_PARAMS,
        name="ffn",
    )(xp, xs, pre.reshape(1, d), post.reshape(1, d), wg, wu, wd, *[w for w, _ in convert])
    return out[0], out[1], out[2:]


def _gla_tile(x_ref, y_ref, s_ref, pre_ref, post_ref, win_ref, wg2_ref, bg_ref, gn_ref, wout_ref,
              q_s, k_s, v_s, g_s, o_s, xn_s, qg_s, kg_s, kdt_s, el_s, a_s, u_s, gate_s,
              *, nb, nc, L,
              side_work=None):
    heads = GLA_HEADS
    key = wg2_ref.shape[1]
    val = wout_ref.shape[0]
    dk, dv = key // heads, val // heads
    r = nb * nc * L
    blk = slice(0, r)

    xn = _rms(x_ref[...], pre_ref[...]).astype(BF16)
    xn_s[blk, :] = xn
    zg = _dot(xn, win_ref[:, 2 * key + 2 * val:])
    q_s[blk, :] = _dot(xn, win_ref[:, 0:key]) * (dk ** -0.5)
    z = _dot(zg.astype(BF16), wg2_ref[...].astype(BF16)) + bg_ref[...]
    k_s[blk, :] = _dot(xn, win_ref[:, key:2 * key])
    g_s[blk, :] = (jnp.minimum(z, 0.0) - jnp.log1p(jnp.exp(-jnp.abs(z)))) / GLA_TAU
    if side_work is not None:
        side_work()
    v_s[blk, :] = _dot(xn, win_ref[:, 2 * key:2 * key + val]).astype(BF16)

    row = lax.broadcasted_iota(jnp.int32, (L, L), 0)
    col = lax.broadcasted_iota(jnp.int32, (L, L), 1)
    causal = row >= col
    tri = jnp.where(causal, 1.0, 0.0).astype(BF16)

    chunks = range(nb * nc)
    rows = [slice(i * L, (i + 1) * L) for i in chunks]
    kslc = [slice(h * dk, (h + 1) * dk) for h in range(heads)]
    vslc = [slice(h * dv, (h + 1) * dv) for h in range(heads)]

    for i in chunks:
        gc = g_s[rows[i], :]
        g1 = gc.astype(BF16)
        rem = gc - g1.astype(F32)
        g2 = rem.astype(BF16)
        g3 = (rem - g2.astype(F32)).astype(BF16)
        g_s[rows[i], :] = _dot(tri, g1) + _dot(tri, g2) + _dot(tri, g3)

    gate_s[blk, :] = _silu(_dot(xn_s[blk, :], win_ref[:, 2 * key + val:2 * key + 2 * val]))

    for i in chunks:
        b = g_s[rows[i], :]
        b_last = b[L - 1:L, :]
        qc = q_s[rows[i], :]
        kc = k_s[rows[i], :]
        qg_s[rows[i], :] = (qc * jnp.exp(b)).astype(BF16)
        kg_s[rows[i], :] = (kc * jnp.exp(-b)).astype(BF16)
        kdt_s[i, :, 0:L] = jnp.transpose(kc * jnp.exp(b_last - b)).astype(BF16)
        el_s[i] = jnp.broadcast_to(jnp.exp(b_last), el_s.shape[1:])

    for i in chunks:
        for h in range(heads):
            a = lax.dot_general(qg_s[rows[i], kslc[h]], kg_s[rows[i], kslc[h]],
                                (((1,), (1,)), ((), ())), preferred_element_type=F32)
            a_s[i * heads + h, 0:L, 0:L] = jnp.where(causal, a, 0.0).astype(BF16)

    for i in chunks:
        for h in range(heads):
            lhs = jnp.concatenate([a_s[i * heads + h, 0:L, 0:L], kdt_s[i, kslc[h], 0:L]], axis=0)
            res = _dot(lhs, v_s[rows[i], vslc[h]])
            o_s[rows[i], vslc[h]] = res[:L]
            u_s[i * heads + h] = res[L:]

    for i in chunks:
        bi = i // nc
        for h in range(heads):
            s_h = s_ref[bi, h]
            o_s[rows[i], vslc[h]] += _dot(qg_s[rows[i], kslc[h]], s_h.astype(BF16))
            decay = jnp.transpose(jnp.broadcast_to(el_s[i, 0:1, kslc[h]], (dk, dk)))
            s_ref[bi, h] = s_h * jnp.tile(decay, (1, dv // dk)) + u_s[i * heads + h]

    gn = gn_ref[...]
    o_n = jnp.concatenate([_rms(o_s[blk, vslc[h]], gn) for h in range(heads)], axis=1)
    y = _dot((o_n * gate_s[blk, :]).astype(BF16), wout_ref[...])
    y_ref[...] = x_ref[...] + _rms(y, post_ref[...])


def _gla_kernel(*refs, n_prompt, tiles_per_seq, chunk_p, batch_s, chunk_s, n_convert):
    xp_ref, xs_ref, s0_ref = refs[:3]
    weights = refs[3:10]
    conv_src = refs[10:10 + n_convert]
    yp_ref, ys_ref, sp_ref, ss_ref = refs[10 + n_convert:14 + n_convert]
    conv_dst = refs[14 + n_convert:14 + 2 * n_convert]
    scratch = refs[14 + 2 * n_convert:]

    def prompt():
        @pl.when(pl.program_id(0) % tiles_per_seq == 0)
        def _():
            sp_ref[...] = jnp.zeros(sp_ref.shape, F32)

        _gla_tile(xp_ref, yp_ref, sp_ref, *weights, *scratch,
                  nb=1, nc=xp_ref.shape[0] // chunk_p, L=chunk_p,
                  side_work=functools.partial(_convert, conv_src, conv_dst))

    def sample():
        ss_ref[...] = s0_ref[...]
        _gla_tile(xs_ref, ys_ref, ss_ref, *weights, *scratch,
                  nb=batch_s, nc=xs_ref.shape[0] // (batch_s * chunk_s), L=chunk_s)

    _two_streams(prompt, sample, n_prompt)


def _gla(xp, xs, s0, pre, post, w_in, w_gate2, b_gate, g_norm, w_out, convert,
         *, batch_p, batch_s):
    mp, d = xp.shape
    ms = xs.shape[0]
    heads = GLA_HEADS
    key = w_gate2.shape[1]
    val = w_out.shape[0]
    seq_p, seq_s = mp // batch_p, ms // batch_s
    chunk_p, chunk_s = min(seq_p, GLA_CHUNK), min(seq_s, GLA_CHUNK)
    tile = TOKEN_TILE
    assert seq_p % tile == 0 and tile % chunk_p == 0 and seq_s % chunk_s == 0 and ms <= tile
    tiles_per_seq = seq_p // tile
    n_prompt = mp // tile
    n_chunks = max(tile // chunk_p, ms // chunk_s)
    chunk = max(chunk_p, chunk_s)
    state_shape = (heads, key // heads, val // heads)
    rows_p = _prompt_rows(tile, d, n_prompt)
    state_p = pl.BlockSpec((1,) + state_shape,
                           lambda i: (jnp.minimum(i, n_prompt - 1) // tiles_per_seq, 0, 0, 0))
    state_s = pl.BlockSpec((batch_s,) + state_shape, lambda i: (0, 0, 0, 0))
    c_in, c_out, c_shape = _convert_specs(convert, n_prompt)
    kern = functools.partial(_gla_kernel, n_prompt=n_prompt, tiles_per_seq=tiles_per_seq,
                             chunk_p=chunk_p, batch_s=batch_s, chunk_s=chunk_s,
                             n_convert=len(convert))
    out = pl.pallas_call(
        kern,
        grid=(n_prompt + 1,),
        in_specs=[rows_p, _resident((ms, d)), _resident((batch_s,) + state_shape),
                  _resident((1, d)), _resident((1, d)), _resident(w_in.shape),
                  _resident(w_gate2.shape), _resident((1, key)), _resident((1, g_norm.shape[0])),
                  _resident(w_out.shape), *c_in],
        out_specs=[rows_p, pl.BlockSpec((ms, d), lambda i: (0, 0)), state_p, state_s, *c_out],
        out_shape=[jax.ShapeDtypeStruct((mp, d), F32), jax.ShapeDtypeStruct((ms, d), F32),
                   jax.ShapeDtypeStruct((batch_p,) + state_shape, F32),
                   jax.ShapeDtypeStruct((batch_s,) + state_shape, F32), *c_shape],
        scratch_shapes=[pltpu.VMEM((tile, key), F32), pltpu.VMEM((tile, key), F32),
                        pltpu.VMEM((tile, val), BF16), pltpu.VMEM((tile, key), F32),
                        pltpu.VMEM((tile, val), F32), pltpu.VMEM((tile, d), BF16),
                        pltpu.VMEM((tile, key), BF16), pltpu.VMEM((tile, key), BF16),
                        pltpu.VMEM((n_chunks, key, chunk), BF16),
                        pltpu.VMEM((n_chunks, 8, key), F32),
                        pltpu.VMEM((n_chunks * heads, chunk, chunk), BF16),
                        pltpu.VMEM((n_chunks * heads,) + state_shape[1:], F32),
                        pltpu.VMEM((tile, val), F32)],
        compiler_params=_PARAMS,
        name="gla",
    )(xp, xs, s0, pre.reshape(1, d), post.reshape(1, d), w_in, w_gate2, b_gate.reshape(1, key),
      g_norm.reshape(1, -1), w_out, *[w for w, _ in convert])
    return out[0], out[1], out[2], out[3], out[4:]


def _sgu_tile(x_ref, y_ref, v_ref, pre_ref, post_ref, win_ref, bin_ref, lng_ref, lnb_ref,
              wmix_ref, bmix_ref, wout_ref, v_s, *, L, side_work=None):
    groups = SGU_GROUPS
    half = wout_ref.shape[0]
    gd = half // groups
    r = x_ref.shape[0]
    blk = slice(0, r)
    win = wmix_ref.shape[1]

    x = x_ref[...]
    xn = _rms(x, pre_ref[...]).astype(BF16)

    s1 = jnp.zeros((r, 1), F32)
    for g in range(groups):
        seg = slice(half + g * gd, half + (g + 1) * gd)
        vg = _gelu(_dot(xn, win_ref[:, seg]) + bin_ref[:, seg])
        v_s[blk, g * gd:(g + 1) * gd] = vg
        s1 = s1 + jnp.sum(vg, axis=-1, keepdims=True)
    mu = s1 / half
    if side_work is not None:
        side_work()

    def u_proj(g):
        seg = slice(g * gd, (g + 1) * gd)
        return _gelu(_dot(xn, win_ref[:, seg]) + bin_ref[:, seg])

    u_ready = [u_proj(g) for g in range(min(2, groups))]
    s2 = jnp.zeros((r, 1), F32)
    for g in range(groups):
        xc = v_s[blk, g * gd:(g + 1) * gd] - mu
        s2 = s2 + jnp.sum(xc * xc, axis=-1, keepdims=True)
    rstd = lax.rsqrt(s2 / half + LN_EPS)

    row = lax.broadcasted_iota(jnp.int32, (win, win), 0)
    col = lax.broadcasted_iota(jnp.int32, (win, win), 1)
    mask = row >= col
    if L < win:
        mask = jnp.logical_and(mask, row // L == col // L)

    d_half = x.shape[1] // 2
    acc = [jnp.zeros((r, d_half), F32), jnp.zeros((r, d_half), F32)]
    for g in range(groups):
        seg = slice(g * gd, (g + 1) * gd)
        if g + 2 < groups:
            u_ready.append(u_proj(g + 2))
        vn = (v_s[blk, seg] - mu) * rstd * lng_ref[:, seg] + lnb_ref[:, seg]
        if v_ref is not None:
            v_ref[:, seg] = vn
        vnb = vn.astype(BF16)
        wm = jnp.where(mask, wmix_ref[g], 0.0).astype(BF16)
        bcol = bmix_ref[:, g:g + 1]
        mixed = jnp.concatenate(
            [_dot(wm, vnb[c * win:(c + 1) * win, :]) + bcol for c in range(r // win)], axis=0)
        p = (u_ready[g] * mixed).astype(BF16)
        acc = [acc[0] + _dot(p, wout_ref[seg, 0:d_half]),
               acc[1] + _dot(p, wout_ref[seg, d_half:])]
    y_ref[...] = x + _rms(jnp.concatenate(acc, axis=1), post_ref[...])


def _sgu_kernel(*refs, n_prompt, chunk_p, chunk_s, n_convert):
    (xp_ref, xs_ref, pre_ref, post_ref, win_ref, bin_ref, lng_ref, lnb_ref,
     wmix_p_ref, bmix_p_ref, wmix_s_ref, bmix_s_ref, wout_ref) = refs[:13]
    conv_src = refs[13:13 + n_convert]
    yp_ref, ys_ref, vs_ref = refs[13 + n_convert:16 + n_convert]
    conv_dst = refs[16 + n_convert:16 + 2 * n_convert]
    (v_s,) = refs[16 + 2 * n_convert:]
    shared = (pre_ref, post_ref, win_ref, bin_ref, lng_ref, lnb_ref)

    _two_streams(
        functools.partial(_sgu_tile, xp_ref, yp_ref, None, *shared, wmix_p_ref, bmix_p_ref,
                          wout_ref, v_s, L=chunk_p,
                          side_work=functools.partial(_convert, conv_src, conv_dst)),
        functools.partial(_sgu_tile, xs_ref, ys_ref, vs_ref, *shared, wmix_s_ref, bmix_s_ref,
                          wout_ref, v_s, L=chunk_s),
        n_prompt)


def _sgu(xp, xs, pre, post, w_in, b_in, ln_g, ln_b, w_s, b_s, w_out, convert, *, seq_p, seq_s):
    mp, d = xp.shape
    ms = xs.shape[0]
    half = w_out.shape[0]
    win = w_s.shape[1]
    chunk_p, chunk_s = min(seq_p, win), min(seq_s, win)
    tile = TOKEN_TILE
    assert chunk_p == win and tile % win == 0 and seq_p % win == 0
    assert ms % win == 0 and win % chunk_s == 0 and seq_s == chunk_s and ms <= tile
    n_prompt = mp // tile
    reps = win // chunk_s
    wmix_s = jnp.tile(w_s[:, :chunk_s, :chunk_s], (1, reps, reps))
    bmix_s = jnp.tile(b_s[:, :chunk_s].T, (reps, 1))
    rows_p = _prompt_rows(tile, d, n_prompt)
    c_in, c_out, c_shape = _convert_specs(convert, n_prompt)
    kern = functools.partial(_sgu_kernel, n_prompt=n_prompt, chunk_p=chunk_p, chunk_s=chunk_s,
                             n_convert=len(convert))
    out = pl.pallas_call(
        kern,
        grid=(n_prompt + 1,),
        in_specs=[rows_p, _resident((ms, d)), _resident((1, d)), _resident((1, d)),
                  _resident(w_in.shape), _resident((1, 2 * half)), _resident((1, half)),
                  _resident((1, half)), _resident(w_s.shape), _resident((win, w_s.shape[0])),
                  _resident(w_s.shape), _resident((win, w_s.shape[0])), _resident(w_out.shape),
                  *c_in],
        out_specs=[rows_p, pl.BlockSpec((ms, d), lambda i: (0, 0)),
                   pl.BlockSpec((ms, half), lambda i: (0, 0)), *c_out],
        out_shape=[jax.ShapeDtypeStruct((mp, d), F32), jax.ShapeDtypeStruct((ms, d), F32),
                   jax.ShapeDtypeStruct((ms, half), F32), *c_shape],
        scratch_shapes=[pltpu.VMEM((tile, half), F32)],
        compiler_params=_PARAMS,
        name="sgu",
    )(xp, xs, pre.reshape(1, d), post.reshape(1, d), w_in, b_in.reshape(1, -1),
      ln_g.reshape(1, -1), ln_b.reshape(1, -1), w_s, b_s.T, wmix_s, bmix_s, w_out,
      *[w for w, _ in convert])
    return out[0], out[1], out[2], out[3:]


def kernel(x_prompt, x_sample, state_gla, norm_pre, norm_post, ffn_w_gate, ffn_w_up, ffn_w_down,
           gla_w_in, gla_w_gate2, gla_b_gate, gla_norm, gla_w_out, sgu_w_in, sgu_b_in, sgu_ln_g,
           sgu_ln_b, sgu_w_s, sgu_b_s, sgu_w_out):
    batch, seq, d = x_prompt.shape
    dbatch, dseq, _ = x_sample.shape
    depth = norm_pre.shape[0]
    xp = x_prompt.reshape(batch * seq, d)
    xs = x_sample.reshape(dbatch * dseq, d)
    gla_p, gla_s, sgu_s = [], [], []

    def ffn_sources(i, k):
        return [(ffn_w_gate, (i, k)), (ffn_w_up, (i, k)), (ffn_w_down, (i, k))]

    sources = []
    for i in range(depth):
        j = i // 2
        mixer = [(gla_w_in, (j,)), (gla_w_out, (j,))] if i % 2 == 0 else \
            [(sgu_w_in, (j,)), (sgu_w_out, (j,))]
        sources += [ffn_sources(i, 0), mixer, ffn_sources(i, 1)]
    sources.append([])

    ready = None
    for i in range(depth):
        j = i // 2
        for part in range(3):
            nxt = sources[3 * i + part + 1]
            if part != 1:
                k, n = (0, 0) if part == 0 else (1, 2)
                if ready is None:
                    w, lead = (ffn_w_gate, ffn_w_up, ffn_w_down), (i, k)
                else:
                    w, lead = ready, None
                xp, xs, ready = _ffn(xp, xs, norm_pre[i, n], norm_post[i, n], *w, lead, nxt)
            elif i % 2 == 0:
                w_in, w_out = ready
                xp, xs, sp, ss, ready = _gla(
                    xp, xs, state_gla[j], norm_pre[i, 1], norm_post[i, 1], w_in, gla_w_gate2[j],
                    gla_b_gate[j], gla_norm[j], w_out, nxt, batch_p=batch, batch_s=dbatch)
                gla_p.append(sp)
                gla_s.append(ss)
            else:
                w_in, w_out = ready
                xp, xs, vs, ready = _sgu(
                    xp, xs, norm_pre[i, 1], norm_post[i, 1], w_in, sgu_b_in[j], sgu_ln_g[j],
                    sgu_ln_b[j], sgu_w_s[j], sgu_b_s[j], w_out, nxt, seq_p=seq, seq_s=dseq)
                sgu_s.append(vs.reshape(dbatch, dseq, -1))

    return (xp.reshape(batch, seq, d), xs.reshape(dbatch, dseq, d),
            jnp.stack(gla_p), jnp.stack(gla_s), jnp.stack(sgu_s))
```

```python
import functools

import jax
import jax.numpy as jnp
from jax import lax
from jax.experimental import pallas as pl
from jax.experimental.pallas import tpu as pltpu

F32 = jnp.float32
BF16 = jnp.bfloat16

NORM_EPS = 1e-6
LN_EPS = 1e-5
GLA_HEADS = 4
GLA_TAU = 16.0
GLA_CHUNK = 64
SGU_CHUNK = 128
SGU_GROUPS = 4

VMEM_LIMIT_BYTES = 56 * 1024 * 1024
F32_SUBLANES = 8
BF16_SUBLANES = 16
TOKEN_TILE = 512
FFN_COL_CHUNK = 512


def _dot(a, b):
    return jnp.dot(a, b, preferred_element_type=F32)


def _rms(x, g, eps=NORM_EPS):
    return x * lax.rsqrt(jnp.mean(x * x, axis=-1, keepdims=True) + eps) * g


def _silu(x):
    return x * jax.nn.sigmoid(x)


def _gelu(x):
    return 0.5 * x * (1.0 + lax.erf(x * (0.5 ** 0.5)))


def _resident(shape):
    zeros = (0,) * len(shape)
    return pl.BlockSpec(shape, lambda *_: zeros, pipeline_mode=pl.Buffered(1))


def _hbm():
    return pl.BlockSpec(memory_space=pl.ANY)


def _prompt_rows(tile, width, n_prompt):
    return pl.BlockSpec((tile, width), lambda i: (jnp.minimum(i, n_prompt - 1), 0))


def _two_streams(body_prompt, body_sample, n_prompt):
    step = pl.program_id(0)
    pl.when(step < n_prompt)(body_prompt)
    pl.when(step == n_prompt)(body_sample)


_PARAMS = pltpu.CompilerParams(dimension_semantics=("arbitrary",),
                               vmem_limit_bytes=VMEM_LIMIT_BYTES)


def _convert_specs(sources, n_prompt):
    in_specs, out_specs, out_shape = [], [], []
    for w, lead in sources:
        rows, cols = w.shape[-2:]
        blk = rows // n_prompt
        assert blk * n_prompt == rows and blk % BF16_SUBLANES == 0
        in_specs.append(pl.BlockSpec(
            (None,) * len(lead) + (blk, cols),
            lambda i, lead=lead: lead + (jnp.minimum(i, n_prompt - 1), 0)))
        out_specs.append(_prompt_rows(blk, cols, n_prompt))
        out_shape.append(jax.ShapeDtypeStruct((rows, cols), BF16))
    return in_specs, out_specs, out_shape


def _convert(src_refs, dst_refs):
    for src, dst in zip(src_refs, dst_refs):
        dst[...] = src[...].astype(BF16)


class _CastStream:
    def __init__(self, pieces, stage, sems):
        self.pieces, self.stage, self.sems = pieces, stage, sems

    def _copy(self, j):
        slot = j % 2
        return pltpu.make_async_copy(self.pieces[j][0], self.stage.at[slot], self.sems.at[slot])

    def start(self, j):
        if j < len(self.pieces):
            self._copy(j).start()

    def land(self, j):
        self._copy(j).wait()
        _, dst, idx = self.pieces[j]
        dst[idx] = self.stage[j % 2].astype(BF16)
        self.start(j + 2)


def _col_pieces(hbm, lead, dst, unit):
    n = dst.shape[1] // unit
    return [(hbm.at[lead + (slice(None), pl.ds(j * unit, unit))], dst,
             (slice(None), slice(j * unit, (j + 1) * unit))) for j in range(n)]


def _row_pieces(hbm, lead, dst, unit):
    n = dst.shape[0] // unit
    return [(hbm.at[lead + (pl.ds(j * unit, unit), slice(None))], dst,
             (slice(j * unit, (j + 1) * unit), slice(None))) for j in range(n)]


def _stage(shape):
    return [pltpu.VMEM((2,) + shape, F32), pltpu.SemaphoreType.DMA((2,))]


def _ffn_tile(x_ref, o_ref, pre_ref, post_ref, wg_ref, wu_ref, wd_ref, before_chunk=None,
              side_work=None):
    x = x_ref[...]
    xn = _rms(x, pre_ref[...]).astype(BF16)
    d_ff = wg_ref.shape[1]
    n_chunks = d_ff // FFN_COL_CHUNK
    cols = [slice(c * FFN_COL_CHUNK, (c + 1) * FFN_COL_CHUNK) for c in range(n_chunks)]

    def gate_up(c):
        if before_chunk is not None:
            before_chunk(c)
        return _dot(xn, wg_ref[:, cols[c]]), _dot(xn, wu_ref[:, cols[c]])

    acc = jnp.zeros(x.shape, F32)
    gu = gate_up(0)
    for c in range(n_chunks):
        gu_next = gate_up(c + 1) if c + 1 < n_chunks else None
        h = _silu(gu[0]) * gu[1]
        acc = acc + _dot(h.astype(BF16), wd_ref[cols[c], :])
        gu = gu_next
        if c == 0 and side_work is not None:
            side_work()
    o_ref[...] = x + _rms(acc, 0.5 * post_ref[...])


def _ffn_kernel(*refs, n_prompt, lead, n_convert):
    xp_ref, xs_ref, pre_ref, post_ref, wg_in, wu_in, wd_in = refs[:7]
    conv_src = refs[7:7 + n_convert]
    op_ref, os_ref = refs[7 + n_convert:9 + n_convert]
    conv_dst = refs[9 + n_convert:9 + 2 * n_convert]
    convert = functools.partial(_convert, conv_src, conv_dst)

    if lead is None:
        weights = (pre_ref, post_ref, wg_in, wu_in, wd_in)
        _two_streams(
            functools.partial(_ffn_tile, xp_ref, op_ref, *weights, side_work=convert),
            functools.partial(_ffn_tile, xs_ref, os_ref, *weights), n_prompt)
        return

    wg_ref, wu_ref, wd_ref, stage_g, sem_g, stage_u, sem_u, stage_d, sem_d = \
        refs[9 + 2 * n_convert:]
    weights = (pre_ref, post_ref, wg_ref, wu_ref, wd_ref)

    def first():
        unit = stage_d.shape[1]
        per_chunk = FFN_COL_CHUNK // unit
        streams = [_CastStream(_col_pieces(wg_in, lead, wg_ref, unit), stage_g, sem_g),
                   _CastStream(_col_pieces(wu_in, lead, wu_ref, unit), stage_u, sem_u),
                   _CastStream(_row_pieces(wd_in, lead, wd_ref, unit), stage_d, sem_d)]
        for s in streams:
            s.start(0)
            s.start(1)

        def land(c):
            for j in range(c * per_chunk, (c + 1) * per_chunk):
                for s in streams:
                    s.land(j)

        _ffn_tile(xp_ref, op_ref, *weights, before_chunk=land, side_work=convert)

    step = pl.program_id(0)
    pl.when(step == 0)(first)
    pl.when(jnp.logical_and(step > 0, step < n_prompt))(
        functools.partial(_ffn_tile, xp_ref, op_ref, *weights, side_work=convert))
    pl.when(step == n_prompt)(functools.partial(_ffn_tile, xs_ref, os_ref, *weights))


def _ffn(xp, xs, pre, post, wg, wu, wd, lead, convert):
    mp, d = xp.shape
    ms = xs.shape[0]
    d_ff = wg.shape[-1]
    n_prompt = mp // TOKEN_TILE
    rows_p = _prompt_rows(TOKEN_TILE, d, n_prompt)
    c_in, c_out, c_shape = _convert_specs(convert, n_prompt)
    if lead is None:
        w_specs = [_resident(wg.shape), _resident(wu.shape), _resident(wd.shape)]
        scratch = []
    else:
        unit = FFN_COL_CHUNK // 2
        w_specs = [_hbm(), _hbm(), _hbm()]
        scratch = [pltpu.VMEM((d, d_ff), BF16), pltpu.VMEM((d, d_ff), BF16),
                   pltpu.VMEM((d_ff, d), BF16), *_stage((d, unit)), *_stage((d, unit)),
                   *_stage((unit, d))]
    out = pl.pallas_call(
        functools.partial(_ffn_kernel, n_prompt=n_prompt, lead=lead, n_convert=len(convert)),
        grid=(n_prompt + 1,),
        in_specs=[rows_p, _resident((ms, d)), _resident((1, d)), _resident((1, d)),
                  *w_specs, *c_in],
        out_specs=[rows_p, pl.BlockSpec((ms, d), lambda i: (0, 0)), *c_out],
        out_shape=[jax.ShapeDtypeStruct((mp, d), F32), jax.ShapeDtypeStruct((ms, d), F32),
                   *c_shape],
        scratch_shapes=scratch,
        compiler_params=_PARAMS,
        name="ffn",
    )(xp, xs, pre.reshape(1, d), post.reshape(1, d), wg, wu, wd, *[w for w, _ in convert])
    return out[0], out[1], out[2:]


def _gla_tile(x_ref, y_ref, s_ref, pre_ref, post_ref, win_ref, wg2_ref, bg_ref, gn_ref, wout_ref,
              q_s, k_s, v_s, g_s, o_s, xn_s, qg_s, kg_s, kdt_s, el_s, a_s, u_s, gate_s,
              *, nb, nc, L,
              side_work=None):
    heads = GLA_HEADS
    key = wg2_ref.shape[1]
    val = wout_ref.shape[0]
    dk, dv = key // heads, val // heads
    r = nb * nc * L
    blk = slice(0, r)

    xn = _rms(x_ref[...], pre_ref[...]).astype(BF16)
    xn_s[blk, :] = xn
    zg = _dot(xn, win_ref[:, 2 * key + 2 * val:])
    q_s[blk, :] = _dot(xn, win_ref[:, 0:key]) * (dk ** -0.5)
    z = _dot(zg.astype(BF16), wg2_ref[...].astype(BF16)) + bg_ref[...]
    k_s[blk, :] = _dot(xn, win_ref[:, key:2 * key])
    g_s[blk, :] = (jnp.minimum(z, 0.0) - jnp.log1p(jnp.exp(-jnp.abs(z)))) / GLA_TAU
    if side_work is not None:
        side_work()
    v_s[blk, :] = _dot(xn, win_ref[:, 2 * key:2 * key + val]).astype(BF16)

    row = lax.broadcasted_iota(jnp.int32, (L, L), 0)
    col = lax.broadcasted_iota(jnp.int32, (L, L), 1)
    causal = row >= col
    tri = jnp.where(causal, 1.0, 0.0).astype(BF16)

    chunks = range(nb * nc)
    rows = [slice(i * L, (i + 1) * L) for i in chunks]
    kslc = [slice(h * dk, (h + 1) * dk) for h in range(heads)]
    vslc = [slice(h * dv, (h + 1) * dv) for h in range(heads)]

    for i in chunks:
        gc = g_s[rows[i], :]
        g1 = gc.astype(BF16)
        rem = gc - g1.astype(F32)
        g2 = rem.astype(BF16)
        g3 = (rem - g2.astype(F32)).astype(BF16)
        g_s[rows[i], :] = _dot(tri, g1) + _dot(tri, g2) + _dot(tri, g3)

    gate_s[blk, :] = _silu(_dot(xn_s[blk, :], win_ref[:, 2 * key + val:2 * key + 2 * val]))

    for i in chunks:
        b = g_s[rows[i], :]
        b_last = b[L - 1:L, :]
        qc = q_s[rows[i], :]
        kc = k_s[rows[i], :]
        qg_s[rows[i], :] = (qc * jnp.exp(b)).astype(BF16)
        kg_s[rows[i], :] = (kc * jnp.exp(-b)).astype(BF16)
        kdt_s[i, :, 0:L] = jnp.transpose(kc * jnp.exp(b_last - b)).astype(BF16)
        el_s[i] = jnp.broadcast_to(jnp.exp(b_last), el_s.shape[1:])

    for i in chunks:
        for h in range(heads):
            a = lax.dot_general(qg_s[rows[i], kslc[h]], kg_s[rows[i], kslc[h]],
                                (((1,), (1,)), ((), ())), preferred_element_type=F32)
            a_s[i * heads + h, 0:L, 0:L] = jnp.where(causal, a, 0.0).astype(BF16)

    for i in chunks:
        for h in range(heads):
            lhs = jnp.concatenate([a_s[i * heads + h, 0:L, 0:L], kdt_s[i, kslc[h], 0:L]], axis=0)
            res = _dot(lhs, v_s[rows[i], vslc[h]])
            o_s[rows[i], vslc[h]] = res[:L]
            u_s[i * heads + h] = res[L:]

    for i in chunks:
        bi = i // nc
        for h in range(heads):
            s_h = s_ref[bi, h]
            o_s[rows[i], vslc[h]] += _dot(qg_s[rows[i], kslc[h]], s_h.astype(BF16))
            decay = jnp.transpose(jnp.broadcast_to(el_s[i, 0:1, kslc[h]], (dk, dk)))
            s_ref[bi, h] = s_h * jnp.tile(decay, (1, dv // dk)) + u_s[i * heads + h]

    gn = gn_ref[...]
    o_n = jnp.concatenate([_rms(o_s[blk, vslc[h]], gn) for h in range(heads)], axis=1)
    y = _dot((o_n * gate_s[blk, :]).astype(BF16), wout_ref[...])
    y_ref[...] = x_ref[...] + _rms(y, post_ref[...])


def _gla_kernel(*refs, n_prompt, tiles_per_seq, chunk_p, batch_s, chunk_s, n_convert):
    xp_ref, xs_ref, s0_ref = refs[:3]
    weights = refs[3:10]
    conv_src = refs[10:10 + n_convert]
    yp_ref, ys_ref, sp_ref, ss_ref = refs[10 + n_convert:14 + n_convert]
    conv_dst = refs[14 + n_convert:14 + 2 * n_convert]
    scratch = refs[14 + 2 * n_convert:]

    def prompt():
        @pl.when(pl.program_id(0) % tiles_per_seq == 0)
        def _():
            sp_ref[...] = jnp.zeros(sp_ref.shape, F32)

        _gla_tile(xp_ref, yp_ref, sp_ref, *weights, *scratch,
                  nb=1, nc=xp_ref.shape[0] // chunk_p, L=chunk_p,
                  side_work=functools.partial(_convert, conv_src, conv_dst))

    def sample():
        ss_ref[...] = s0_ref[...]
        _gla_tile(xs_ref, ys_ref, ss_ref, *weights, *scratch,
                  nb=batch_s, nc=xs_ref.shape[0] // (batch_s * chunk_s), L=chunk_s)

    _two_streams(prompt, sample, n_prompt)


def _gla(xp, xs, s0, pre, post, w_in, w_gate2, b_gate, g_norm, w_out, convert,
         *, batch_p, batch_s):
    mp, d = xp.shape
    ms = xs.shape[0]
    heads = GLA_HEADS
    key = w_gate2.shape[1]
    val = w_out.shape[0]
    seq_p, seq_s = mp // batch_p, ms // batch_s
    chunk_p, chunk_s = min(seq_p, GLA_CHUNK), min(seq_s, GLA_CHUNK)
    tile = TOKEN_TILE
    assert seq_p % tile == 0 and tile % chunk_p == 0 and seq_s % chunk_s == 0 and ms <= tile
    tiles_per_seq = seq_p // tile
    n_prompt = mp // tile
    n_chunks = max(tile // chunk_p, ms // chunk_s)
    chunk = max(chunk_p, chunk_s)
    state_shape = (heads, key // heads, val // heads)
    rows_p = _prompt_rows(tile, d, n_prompt)
    state_p = pl.BlockSpec((1,) + state_shape,
                           lambda i: (jnp.minimum(i, n_prompt - 1) // tiles_per_seq, 0, 0, 0))
    state_s = pl.BlockSpec((batch_s,) + state_shape, lambda i: (0, 0, 0, 0))
    c_in, c_out, c_shape = _convert_specs(convert, n_prompt)
    kern = functools.partial(_gla_kernel, n_prompt=n_prompt, tiles_per_seq=tiles_per_seq,
                             chunk_p=chunk_p, batch_s=batch_s, chunk_s=chunk_s,
                             n_convert=len(convert))
    out = pl.pallas_call(
        kern,
        grid=(n_prompt + 1,),
        in_specs=[rows_p, _resident((ms, d)), _resident((batch_s,) + state_shape),
                  _resident((1, d)), _resident((1, d)), _resident(w_in.shape),
                  _resident(w_gate2.shape), _resident((1, key)), _resident((1, g_norm.shape[0])),
                  _resident(w_out.shape), *c_in],
        out_specs=[rows_p, pl.BlockSpec((ms, d), lambda i: (0, 0)), state_p, state_s, *c_out],
        out_shape=[jax.ShapeDtypeStruct((mp, d), F32), jax.ShapeDtypeStruct((ms, d), F32),
                   jax.ShapeDtypeStruct((batch_p,) + state_shape, F32),
                   jax.ShapeDtypeStruct((batch_s,) + state_shape, F32), *c_shape],
        scratch_shapes=[pltpu.VMEM((tile, key), F32), pltpu.VMEM((tile, key), F32),
                        pltpu.VMEM((tile, val), BF16), pltpu.VMEM((tile, key), F32),
                        pltpu.VMEM((tile, val), F32), pltpu.VMEM((tile, d), BF16),
                        pltpu.VMEM((tile, key), BF16), pltpu.VMEM((tile, key), BF16),
                        pltpu.VMEM((n_chunks, key, chunk), BF16),
                        pltpu.VMEM((n_chunks, F32_SUBLANES, key), F32),
                        pltpu.VMEM((n_chunks * heads, chunk, chunk), BF16),
                        pltpu.VMEM((n_chunks * heads,) + state_shape[1:], F32),
                        pltpu.VMEM((tile, val), F32)],
        compiler_params=_PARAMS,
        name="gla",
    )(xp, xs, s0, pre.reshape(1, d), post.reshape(1, d), w_in, w_gate2, b_gate.reshape(1, key),
      g_norm.reshape(1, -1), w_out, *[w for w, _ in convert])
    return out[0], out[1], out[2], out[3], out[4:]


def _sgu_tile(x_ref, y_ref, v_ref, pre_ref, post_ref, win_ref, bin_ref, lng_ref, lnb_ref,
              wmix_ref, bmix_ref, wout_ref, v_s, *, L, side_work=None):
    groups = SGU_GROUPS
    half = wout_ref.shape[0]
    gd = half // groups
    r = x_ref.shape[0]
    blk = slice(0, r)
    win = wmix_ref.shape[1]

    x = x_ref[...]
    xn = _rms(x, pre_ref[...]).astype(BF16)

    s1 = jnp.zeros((r, 1), F32)
    for g in range(groups):
        seg = slice(half + g * gd, half + (g + 1) * gd)
        vg = _gelu(_dot(xn, win_ref[:, seg]) + bin_ref[:, seg])
        v_s[blk, g * gd:(g + 1) * gd] = vg
        s1 = s1 + jnp.sum(vg, axis=-1, keepdims=True)
    mu = s1 / half
    if side_work is not None:
        side_work()

    def u_proj(g):
        seg = slice(g * gd, (g + 1) * gd)
        return _gelu(_dot(xn, win_ref[:, seg]) + bin_ref[:, seg])

    u_ready = [u_proj(g) for g in range(min(2, groups))]
    s2 = jnp.zeros((r, 1), F32)
    for g in range(groups):
        xc = v_s[blk, g * gd:(g + 1) * gd] - mu
        s2 = s2 + jnp.sum(xc * xc, axis=-1, keepdims=True)
    rstd = lax.rsqrt(s2 / half + LN_EPS)

    row = lax.broadcasted_iota(jnp.int32, (win, win), 0)
    col = lax.broadcasted_iota(jnp.int32, (win, win), 1)
    mask = row >= col
    if L < win:
        mask = jnp.logical_and(mask, row // L == col // L)

    d_half = x.shape[1] // 2
    acc = [jnp.zeros((r, d_half), F32), jnp.zeros((r, d_half), F32)]
    for g in range(groups):
        seg = slice(g * gd, (g + 1) * gd)
        if g + 2 < groups:
            u_ready.append(u_proj(g + 2))
        vn = (v_s[blk, seg] - mu) * rstd * lng_ref[:, seg] + lnb_ref[:, seg]
        if v_ref is not None:
            v_ref[:, seg] = vn
        vnb = vn.astype(BF16)
        wm = jnp.where(mask, wmix_ref[g], 0.0).astype(BF16)
        bcol = bmix_ref[:, g:g + 1]
        mixed = jnp.concatenate(
            [_dot(wm, vnb[c * win:(c + 1) * win, :]) + bcol for c in range(r // win)], axis=0)
        p = (u_ready[g] * mixed).astype(BF16)
        acc = [acc[0] + _dot(p, wout_ref[seg, 0:d_half]),
               acc[1] + _dot(p, wout_ref[seg, d_half:])]
    y_ref[...] = x + _rms(jnp.concatenate(acc, axis=1), post_ref[...])


def _sgu_kernel(*refs, n_prompt, chunk_p, chunk_s, n_convert):
    (xp_ref, xs_ref, pre_ref, post_ref, win_ref, bin_ref, lng_ref, lnb_ref,
     wmix_p_ref, bmix_p_ref, wmix_s_ref, bmix_s_ref, wout_ref) = refs[:13]
    conv_src = refs[13:13 + n_convert]
    yp_ref, ys_ref, vs_ref = refs[13 + n_convert:16 + n_convert]
    conv_dst = refs[16 + n_convert:16 + 2 * n_convert]
    (v_s,) = refs[16 + 2 * n_convert:]
    shared = (pre_ref, post_ref, win_ref, bin_ref, lng_ref, lnb_ref)

    _two_streams(
        functools.partial(_sgu_tile, xp_ref, yp_ref, None, *shared, wmix_p_ref, bmix_p_ref,
                          wout_ref, v_s, L=chunk_p,
                          side_work=functools.partial(_convert, conv_src, conv_dst)),
        functools.partial(_sgu_tile, xs_ref, ys_ref, vs_ref, *shared, wmix_s_ref, bmix_s_ref,
                          wout_ref, v_s, L=chunk_s),
        n_prompt)


def _sgu(xp, xs, pre, post, w_in, b_in, ln_g, ln_b, w_s, b_s, w_out, convert, *, seq_p, seq_s):
    mp, d = xp.shape
    ms = xs.shape[0]
    half = w_out.shape[0]
    win = w_s.shape[1]
    chunk_p, chunk_s = min(seq_p, win), min(seq_s, win)
    tile = TOKEN_TILE
    assert chunk_p == win and tile % win == 0 and seq_p % win == 0
    assert ms % win == 0 and win % chunk_s == 0 and seq_s == chunk_s and ms <= tile
    n_prompt = mp // tile
    reps = win // chunk_s
    wmix_s = jnp.tile(w_s[:, :chunk_s, :chunk_s], (1, reps, reps))
    bmix_s = jnp.tile(b_s[:, :chunk_s].T, (reps, 1))
    rows_p = _prompt_rows(tile, d, n_prompt)
    c_in, c_out, c_shape = _convert_specs(convert, n_prompt)
    kern = functools.partial(_sgu_kernel, n_prompt=n_prompt, chunk_p=chunk_p, chunk_s=chunk_s,
                             n_convert=len(convert))
    out = pl.pallas_call(
        kern,
        grid=(n_prompt + 1,),
        in_specs=[rows_p, _resident((ms, d)), _resident((1, d)), _resident((1, d)),
                  _resident(w_in.shape), _resident((1, 2 * half)), _resident((1, half)),
                  _resident((1, half)), _resident(w_s.shape), _resident((win, w_s.shape[0])),
                  _resident(w_s.shape), _resident((win, w_s.shape[0])), _resident(w_out.shape),
                  *c_in],
        out_specs=[rows_p, pl.BlockSpec((ms, d), lambda i: (0, 0)),
                   pl.BlockSpec((ms, half), lambda i: (0, 0)), *c_out],
        out_shape=[jax.ShapeDtypeStruct((mp, d), F32), jax.ShapeDtypeStruct((ms, d), F32),
                   jax.ShapeDtypeStruct((ms, half), F32), *c_shape],
        scratch_shapes=[pltpu.VMEM((tile, half), F32)],
        compiler_params=_PARAMS,
        name="sgu",
    )(xp, xs, pre.reshape(1, d), post.reshape(1, d), w_in, b_in.reshape(1, -1),
      ln_g.reshape(1, -1), ln_b.reshape(1, -1), w_s, b_s.T, wmix_s, bmix_s, w_out,
      *[w for w, _ in convert])
    return out[0], out[1], out[2], out[3:]


def kernel(x_prompt, x_sample, state_gla, norm_pre, norm_post, ffn_w_gate, ffn_w_up, ffn_w_down,
           gla_w_in, gla_w_gate2, gla_b_gate, gla_norm, gla_w_out, sgu_w_in, sgu_b_in, sgu_ln_g,
           sgu_ln_b, sgu_w_s, sgu_b_s, sgu_w_out):
    batch, seq, d = x_prompt.shape
    dbatch, dseq, _ = x_sample.shape
    depth = norm_pre.shape[0]
    xp = x_prompt.reshape(batch * seq, d)
    xs = x_sample.reshape(dbatch * dseq, d)
    gla_p, gla_s, sgu_s = [], [], []

    def ffn_sources(i, k):
        return [(ffn_w_gate, (i, k)), (ffn_w_up, (i, k)), (ffn_w_down, (i, k))]

    sources = []
    for i in range(depth):
        j = i // 2
        mixer = [(gla_w_in, (j,)), (gla_w_out, (j,))] if i % 2 == 0 else \
            [(sgu_w_in, (j,)), (sgu_w_out, (j,))]
        sources += [ffn_sources(i, 0), mixer, ffn_sources(i, 1)]
    sources.append([])

    ready = None
    for i in range(depth):
        j = i // 2
        for part in range(3):
            nxt = sources[3 * i + part + 1]
            if part != 1:
                k, n = (0, 0) if part == 0 else (1, 2)
                if ready is None:
                    w, lead = (ffn_w_gate, ffn_w_up, ffn_w_down), (i, k)
                else:
                    w, lead = ready, None
                xp, xs, ready = _ffn(xp, xs, norm_pre[i, n], norm_post[i, n], *w, lead, nxt)
            elif i % 2 == 0:
                w_in, w_out = ready
                xp, xs, sp, ss, ready = _gla(
                    xp, xs, state_gla[j], norm_pre[i, 1], norm_post[i, 1], w_in, gla_w_gate2[j],
                    gla_b_gate[j], gla_norm[j], w_out, nxt, batch_p=batch, batch_s=dbatch)
                gla_p.append(sp)
                gla_s.append(ss)
            else:
                w_in, w_out = ready
                xp, xs, vs, ready = _sgu(
                    xp, xs, norm_pre[i, 1], norm_post[i, 1], w_in, sgu_b_in[j], sgu_ln_g[j],
                    sgu_ln_b[j], sgu_w_s[j], sgu_b_s[j], w_out, nxt, seq_p=seq, seq_s=dseq)
                sgu_s.append(vs.reshape(dbatch, dseq, -1))

    return (xp.reshape(batch, seq, d), xs.reshape(dbatch, dseq, d),
            jnp.stack(gla_p), jnp.stack(gla_s), jnp.stack(sgu_s))
```

```python
import functools

import jax
import jax.numpy as jnp
from jax import lax
from jax.experimental import pallas as pl
from jax.experimental.pallas import tpu as pltpu

F32 = jnp.float32
BF16 = jnp.bfloat16

NORM_EPS = 1e-6
LN_EPS = 1e-5
GLA_HEADS = 4
GLA_TAU = 16.0
GLA_CHUNK = 64
SGU_CHUNK = 128
SGU_GROUPS = 4

VMEM_LIMIT_BYTES = 56 * 1024 * 1024
F32_SUBLANES = 8
BF16_SUBLANES = 16
TOKEN_TILE = 512
FFN_COL_CHUNK = 256


def _dot(a, b):
    return jnp.dot(a, b, preferred_element_type=F32)


def _rms(x, g, eps=NORM_EPS):
    return x * lax.rsqrt(jnp.mean(x * x, axis=-1, keepdims=True) + eps) * g


def _silu(x):
    return x * jax.nn.sigmoid(x)


def _gelu(x):
    return 0.5 * x * (1.0 + lax.erf(x * (0.5 ** 0.5)))


def _resident(shape):
    zeros = (0,) * len(shape)
    return pl.BlockSpec(shape, lambda *_: zeros, pipeline_mode=pl.Buffered(1))


def _hbm():
    return pl.BlockSpec(memory_space=pl.ANY)


def _prompt_rows(tile, width, n_prompt):
    return pl.BlockSpec((tile, width), lambda i: (jnp.minimum(i, n_prompt - 1), 0))


def _two_streams(body_prompt, body_sample, n_prompt):
    step = pl.program_id(0)
    pl.when(step < n_prompt)(body_prompt)
    pl.when(step == n_prompt)(body_sample)


_PARAMS = pltpu.CompilerParams(dimension_semantics=("arbitrary",),
                               vmem_limit_bytes=VMEM_LIMIT_BYTES)


def _convert_specs(sources, n_prompt):
    in_specs, out_specs, out_shape = [], [], []
    for w, lead in sources:
        rows, cols = w.shape[-2:]
        blk = rows // n_prompt
        assert blk * n_prompt == rows and blk % BF16_SUBLANES == 0
        in_specs.append(pl.BlockSpec(
            (None,) * len(lead) + (blk, cols),
            lambda i, lead=lead: lead + (jnp.minimum(i, n_prompt - 1), 0)))
        out_specs.append(_prompt_rows(blk, cols, n_prompt))
        out_shape.append(jax.ShapeDtypeStruct((rows, cols), BF16))
    return in_specs, out_specs, out_shape


def _convert(src_refs, dst_refs):
    for src, dst in zip(src_refs, dst_refs):
        dst[...] = src[...].astype(BF16)


class _CastStream:
    def __init__(self, pieces, stage, sems):
        self.pieces, self.stage, self.sems = pieces, stage, sems

    def _copy(self, j):
        slot = j % 2
        return pltpu.make_async_copy(self.pieces[j][0], self.stage.at[slot], self.sems.at[slot])

    def start(self, j):
        if j < len(self.pieces):
            self._copy(j).start()

    def land(self, j):
        self._copy(j).wait()
        _, dst, idx = self.pieces[j]
        dst[idx] = self.stage[j % 2].astype(BF16)
        self.start(j + 2)


def _col_pieces(hbm, lead, dst, unit):
    n = dst.shape[1] // unit
    return [(hbm.at[lead + (slice(None), pl.ds(j * unit, unit))], dst,
             (slice(None), slice(j * unit, (j + 1) * unit))) for j in range(n)]


def _row_pieces(hbm, lead, dst, unit):
    n = dst.shape[0] // unit
    return [(hbm.at[lead + (pl.ds(j * unit, unit), slice(None))], dst,
             (slice(j * unit, (j + 1) * unit), slice(None))) for j in range(n)]


def _stage(shape):
    return [pltpu.VMEM((2,) + shape, F32), pltpu.SemaphoreType.DMA((2,))]


def _ffn_tile(x_ref, o_ref, pre_ref, post_ref, wg_ref, wu_ref, wd_ref, before_chunk=None,
              side_work=None):
    x = x_ref[...]
    xn = _rms(x, pre_ref[...]).astype(BF16)
    d_ff = wg_ref.shape[1]
    n_chunks = d_ff // FFN_COL_CHUNK
    cols = [slice(c * FFN_COL_CHUNK, (c + 1) * FFN_COL_CHUNK) for c in range(n_chunks)]

    def gate_up(c):
        if before_chunk is not None:
            before_chunk(c)
        return _dot(xn, wg_ref[:, cols[c]]), _dot(xn, wu_ref[:, cols[c]])

    acc = jnp.zeros(x.shape, F32)
    gu = gate_up(0)
    for c in range(n_chunks):
        gu_next = gate_up(c + 1) if c + 1 < n_chunks else None
        h = _silu(gu[0]) * gu[1]
        acc = acc + _dot(h.astype(BF16), wd_ref[cols[c], :])
        gu = gu_next
        if c == 0 and side_work is not None:
            side_work()
    o_ref[...] = x + _rms(acc, 0.5 * post_ref[...])


def _ffn_kernel(*refs, n_prompt, lead, n_convert):
    xp_ref, xs_ref, pre_ref, post_ref, wg_in, wu_in, wd_in = refs[:7]
    conv_src = refs[7:7 + n_convert]
    op_ref, os_ref = refs[7 + n_convert:9 + n_convert]
    conv_dst = refs[9 + n_convert:9 + 2 * n_convert]
    convert = functools.partial(_convert, conv_src, conv_dst)

    if lead is None:
        weights = (pre_ref, post_ref, wg_in, wu_in, wd_in)
        _two_streams(
            functools.partial(_ffn_tile, xp_ref, op_ref, *weights, side_work=convert),
            functools.partial(_ffn_tile, xs_ref, os_ref, *weights), n_prompt)
        return

    wg_ref, wu_ref, wd_ref, stage_g, sem_g, stage_u, sem_u, stage_d, sem_d = \
        refs[9 + 2 * n_convert:]
    weights = (pre_ref, post_ref, wg_ref, wu_ref, wd_ref)

    def first():
        unit = stage_d.shape[1]
        per_chunk = FFN_COL_CHUNK // unit
        streams = [_CastStream(_col_pieces(wg_in, lead, wg_ref, unit), stage_g, sem_g),
                   _CastStream(_col_pieces(wu_in, lead, wu_ref, unit), stage_u, sem_u),
                   _CastStream(_row_pieces(wd_in, lead, wd_ref, unit), stage_d, sem_d)]
        for s in streams:
            s.start(0)
            s.start(1)

        def land(c):
            for j in range(c * per_chunk, (c + 1) * per_chunk):
                for s in streams:
                    s.land(j)

        _ffn_tile(xp_ref, op_ref, *weights, before_chunk=land, side_work=convert)

    step = pl.program_id(0)
    pl.when(step == 0)(first)
    pl.when(jnp.logical_and(step > 0, step < n_prompt))(
        functools.partial(_ffn_tile, xp_ref, op_ref, *weights, side_work=convert))
    pl.when(step == n_prompt)(functools.partial(_ffn_tile, xs_ref, os_ref, *weights))


def _ffn(xp, xs, pre, post, wg, wu, wd, lead, convert):
    mp, d = xp.shape
    ms = xs.shape[0]
    d_ff = wg.shape[-1]
    n_prompt = mp // TOKEN_TILE
    rows_p = _prompt_rows(TOKEN_TILE, d, n_prompt)
    c_in, c_out, c_shape = _convert_specs(convert, n_prompt)
    if lead is None:
        w_specs = [_resident(wg.shape), _resident(wu.shape), _resident(wd.shape)]
        scratch = []
    else:
        unit = FFN_COL_CHUNK // 2
        w_specs = [_hbm(), _hbm(), _hbm()]
        scratch = [pltpu.VMEM((d, d_ff), BF16), pltpu.VMEM((d, d_ff), BF16),
                   pltpu.VMEM((d_ff, d), BF16), *_stage((d, unit)), *_stage((d, unit)),
                   *_stage((unit, d))]
    out = pl.pallas_call(
        functools.partial(_ffn_kernel, n_prompt=n_prompt, lead=lead, n_convert=len(convert)),
        grid=(n_prompt + 1,),
        in_specs=[rows_p, _resident((ms, d)), _resident((1, d)), _resident((1, d)),
                  *w_specs, *c_in],
        out_specs=[rows_p, pl.BlockSpec((ms, d), lambda i: (0, 0)), *c_out],
        out_shape=[jax.ShapeDtypeStruct((mp, d), F32), jax.ShapeDtypeStruct((ms, d), F32),
                   *c_shape],
        scratch_shapes=scratch,
        compiler_params=_PARAMS,
        name="ffn",
    )(xp, xs, pre.reshape(1, d), post.reshape(1, d), wg, wu, wd, *[w for w, _ in convert])
    return out[0], out[1], out[2:]


def _gla_tile(x_ref, y_ref, s_ref, pre_ref, post_ref, win_ref, wg2_ref, bg_ref, gn_ref, wout_ref,
              q_s, k_s, v_s, g_s, o_s, xn_s, qg_s, kg_s, kdt_s, el_s, a_s, u_s, gate_s,
              *, nb, nc, L,
              side_work=None):
    heads = GLA_HEADS
    key = wg2_ref.shape[1]
    val = wout_ref.shape[0]
    dk, dv = key // heads, val // heads
    r = nb * nc * L
    blk = slice(0, r)

    xn = _rms(x_ref[...], pre_ref[...]).astype(BF16)
    xn_s[blk, :] = xn
    zg = _dot(xn, win_ref[:, 2 * key + 2 * val:])
    q_s[blk, :] = _dot(xn, win_ref[:, 0:key]) * (dk ** -0.5)
    z = _dot(zg.astype(BF16), wg2_ref[...].astype(BF16)) + bg_ref[...]
    k_s[blk, :] = _dot(xn, win_ref[:, key:2 * key])
    g_s[blk, :] = (jnp.minimum(z, 0.0) - jnp.log1p(jnp.exp(-jnp.abs(z)))) / GLA_TAU
    if side_work is not None:
        side_work()
    v_s[blk, :] = _dot(xn, win_ref[:, 2 * key:2 * key + val]).astype(BF16)

    row = lax.broadcasted_iota(jnp.int32, (L, L), 0)
    col = lax.broadcasted_iota(jnp.int32, (L, L), 1)
    causal = row >= col
    tri = jnp.where(causal, 1.0, 0.0).astype(BF16)

    chunks = range(nb * nc)
    rows = [slice(i * L, (i + 1) * L) for i in chunks]
    kslc = [slice(h * dk, (h + 1) * dk) for h in range(heads)]
    vslc = [slice(h * dv, (h + 1) * dv) for h in range(heads)]

    for i in chunks:
        gc = g_s[rows[i], :]
        g1 = gc.astype(BF16)
        rem = gc - g1.astype(F32)
        g2 = rem.astype(BF16)
        g3 = (rem - g2.astype(F32)).astype(BF16)
        g_s[rows[i], :] = _dot(tri, g1) + _dot(tri, g2) + _dot(tri, g3)

    gate_s[blk, :] = _silu(_dot(xn_s[blk, :], win_ref[:, 2 * key + val:2 * key + 2 * val]))

    for i in chunks:
        b = g_s[rows[i], :]
        b_last = b[L - 1:L, :]
        qc = q_s[rows[i], :]
        kc = k_s[rows[i], :]
        qg_s[rows[i], :] = (qc * jnp.exp(b)).astype(BF16)
        kg_s[rows[i], :] = (kc * jnp.exp(-b)).astype(BF16)
        kdt_s[i, :, 0:L] = jnp.transpose(kc * jnp.exp(b_last - b)).astype(BF16)
        el_s[i] = jnp.broadcast_to(jnp.exp(b_last), el_s.shape[1:])

    for i in chunks:
        for h in range(heads):
            a = lax.dot_general(qg_s[rows[i], kslc[h]], kg_s[rows[i], kslc[h]],
                                (((1,), (1,)), ((), ())), preferred_element_type=F32)
            a_s[i * heads + h, 0:L, 0:L] = jnp.where(causal, a, 0.0).astype(BF16)

    for i in chunks:
        for h in range(heads):
            lhs = jnp.concatenate([a_s[i * heads + h, 0:L, 0:L], kdt_s[i, kslc[h], 0:L]], axis=0)
            res = _dot(lhs, v_s[rows[i], vslc[h]])
            o_s[rows[i], vslc[h]] = res[:L]
            u_s[i * heads + h] = res[L:]

    for i in chunks:
        bi = i // nc
        for h in range(heads):
            s_h = s_ref[bi, h]
            o_s[rows[i], vslc[h]] += _dot(qg_s[rows[i], kslc[h]], s_h.astype(BF16))
            decay = jnp.transpose(jnp.broadcast_to(el_s[i, 0:1, kslc[h]], (dk, dk)))
            s_ref[bi, h] = s_h * jnp.tile(decay, (1, dv // dk)) + u_s[i * heads + h]

    gn = gn_ref[...]
    o_n = jnp.concatenate([_rms(o_s[blk, vslc[h]], gn) for h in range(heads)], axis=1)
    y = _dot((o_n * gate_s[blk, :]).astype(BF16), wout_ref[...])
    y_ref[...] = x_ref[...] + _rms(y, post_ref[...])


def _gla_kernel(*refs, n_prompt, tiles_per_seq, chunk_p, batch_s, chunk_s, n_convert):
    xp_ref, xs_ref, s0_ref = refs[:3]
    weights = refs[3:10]
    conv_src = refs[10:10 + n_convert]
    yp_ref, ys_ref, sp_ref, ss_ref = refs[10 + n_convert:14 + n_convert]
    conv_dst = refs[14 + n_convert:14 + 2 * n_convert]
    scratch = refs[14 + 2 * n_convert:]

    def prompt():
        @pl.when(pl.program_id(0) % tiles_per_seq == 0)
        def _():
            sp_ref[...] = jnp.zeros(sp_ref.shape, F32)

        _gla_tile(xp_ref, yp_ref, sp_ref, *weights, *scratch,
                  nb=1, nc=xp_ref.shape[0] // chunk_p, L=chunk_p,
                  side_work=functools.partial(_convert, conv_src, conv_dst))

    def sample():
        ss_ref[...] = s0_ref[...]
        _gla_tile(xs_ref, ys_ref, ss_ref, *weights, *scratch,
                  nb=batch_s, nc=xs_ref.shape[0] // (batch_s * chunk_s), L=chunk_s)

    _two_streams(prompt, sample, n_prompt)


def _gla(xp, xs, s0, pre, post, w_in, w_gate2, b_gate, g_norm, w_out, convert,
         *, batch_p, batch_s):
    mp, d = xp.shape
    ms = xs.shape[0]
    heads = GLA_HEADS
    key = w_gate2.shape[1]
    val = w_out.shape[0]
    seq_p, seq_s = mp // batch_p, ms // batch_s
    chunk_p, chunk_s = min(seq_p, GLA_CHUNK), min(seq_s, GLA_CHUNK)
    tile = TOKEN_TILE
    assert seq_p % tile == 0 and tile % chunk_p == 0 and seq_s % chunk_s == 0 and ms <= tile
    tiles_per_seq = seq_p // tile
    n_prompt = mp // tile
    n_chunks = max(tile // chunk_p, ms // chunk_s)
    chunk = max(chunk_p, chunk_s)
    state_shape = (heads, key // heads, val // heads)
    rows_p = _prompt_rows(tile, d, n_prompt)
    state_p = pl.BlockSpec((1,) + state_shape,
                           lambda i: (jnp.minimum(i, n_prompt - 1) // tiles_per_seq, 0, 0, 0))
    state_s = pl.BlockSpec((batch_s,) + state_shape, lambda i: (0, 0, 0, 0))
    c_in, c_out, c_shape = _convert_specs(convert, n_prompt)
    kern = functools.partial(_gla_kernel, n_prompt=n_prompt, tiles_per_seq=tiles_per_seq,
                             chunk_p=chunk_p, batch_s=batch_s, chunk_s=chunk_s,
                             n_convert=len(convert))
    out = pl.pallas_call(
        kern,
        grid=(n_prompt + 1,),
        in_specs=[rows_p, _resident((ms, d)), _resident((batch_s,) + state_shape),
                  _resident((1, d)), _resident((1, d)), _resident(w_in.shape),
                  _resident(w_gate2.shape), _resident((1, key)), _resident((1, g_norm.shape[0])),
                  _resident(w_out.shape), *c_in],
        out_specs=[rows_p, pl.BlockSpec((ms, d), lambda i: (0, 0)), state_p, state_s, *c_out],
        out_shape=[jax.ShapeDtypeStruct((mp, d), F32), jax.ShapeDtypeStruct((ms, d), F32),
                   jax.ShapeDtypeStruct((batch_p,) + state_shape, F32),
                   jax.ShapeDtypeStruct((batch_s,) + state_shape, F32), *c_shape],
        scratch_shapes=[pltpu.VMEM((tile, key), F32), pltpu.VMEM((tile, key), F32),
                        pltpu.VMEM((tile, val), BF16), pltpu.VMEM((tile, key), F32),
                        pltpu.VMEM((tile, val), F32), pltpu.VMEM((tile, d), BF16),
                        pltpu.VMEM((tile, key), BF16), pltpu.VMEM((tile, key), BF16),
                        pltpu.VMEM((n_chunks, key, chunk), BF16),
                        pltpu.VMEM((n_chunks, F32_SUBLANES, key), F32),
                        pltpu.VMEM((n_chunks * heads, chunk, chunk), BF16),
                        pltpu.VMEM((n_chunks * heads,) + state_shape[1:], F32),
                        pltpu.VMEM((tile, val), F32)],
        compiler_params=_PARAMS,
        name="gla",
    )(xp, xs, s0, pre.reshape(1, d), post.reshape(1, d), w_in, w_gate2, b_gate.reshape(1, key),
      g_norm.reshape(1, -1), w_out, *[w for w, _ in convert])
    return out[0], out[1], out[2], out[3], out[4:]


def _sgu_tile(x_ref, y_ref, v_ref, pre_ref, post_ref, win_ref, bin_ref, lng_ref, lnb_ref,
              wmix_ref, bmix_ref, wout_ref, v_s, *, L, side_work=None):
    groups = SGU_GROUPS
    half = wout_ref.shape[0]
    gd = half // groups
    r = x_ref.shape[0]
    blk = slice(0, r)
    win = wmix_ref.shape[1]

    x = x_ref[...]
    xn = _rms(x, pre_ref[...]).astype(BF16)

    s1 = jnp.zeros((r, 1), F32)
    for g in range(groups):
        seg = slice(half + g * gd, half + (g + 1) * gd)
        vg = _gelu(_dot(xn, win_ref[:, seg]) + bin_ref[:, seg])
        v_s[blk, g * gd:(g + 1) * gd] = vg
        s1 = s1 + jnp.sum(vg, axis=-1, keepdims=True)
    mu = s1 / half
    if side_work is not None:
        side_work()

    def u_proj(g):
        seg = slice(g * gd, (g + 1) * gd)
        return _gelu(_dot(xn, win_ref[:, seg]) + bin_ref[:, seg])

    u_ready = [u_proj(g) for g in range(min(2, groups))]
    s2 = jnp.zeros((r, 1), F32)
    for g in range(groups):
        xc = v_s[blk, g * gd:(g + 1) * gd] - mu
        s2 = s2 + jnp.sum(xc * xc, axis=-1, keepdims=True)
    rstd = lax.rsqrt(s2 / half + LN_EPS)

    row = lax.broadcasted_iota(jnp.int32, (win, win), 0)
    col = lax.broadcasted_iota(jnp.int32, (win, win), 1)
    mask = row >= col
    if L < win:
        mask = jnp.logical_and(mask, row // L == col // L)

    d_half = x.shape[1] // 2
    acc = [jnp.zeros((r, d_half), F32), jnp.zeros((r, d_half), F32)]
    for g in range(groups):
        seg = slice(g * gd, (g + 1) * gd)
        if g + 2 < groups:
            u_ready.append(u_proj(g + 2))
        vn = (v_s[blk, seg] - mu) * rstd * lng_ref[:, seg] + lnb_ref[:, seg]
        if v_ref is not None:
            v_ref[:, seg] = vn
        vnb = vn.astype(BF16)
        wm = jnp.where(mask, wmix_ref[g], 0.0).astype(BF16)
        bcol = bmix_ref[:, g:g + 1]
        mixed = jnp.concatenate(
            [_dot(wm, vnb[c * win:(c + 1) * win, :]) + bcol for c in range(r // win)], axis=0)
        p = (u_ready[g] * mixed).astype(BF16)
        acc = [acc[0] + _dot(p, wout_ref[seg, 0:d_half]),
               acc[1] + _dot(p, wout_ref[seg, d_half:])]
    y_ref[...] = x + _rms(jnp.concatenate(acc, axis=1), post_ref[...])


def _sgu_kernel(*refs, n_prompt, chunk_p, chunk_s, n_convert):
    (xp_ref, xs_ref, pre_ref, post_ref, win_ref, bin_ref, lng_ref, lnb_ref,
     wmix_p_ref, bmix_p_ref, wmix_s_ref, bmix_s_ref, wout_ref) = refs[:13]
    conv_src = refs[13:13 + n_convert]
    yp_ref, ys_ref, vs_ref = refs[13 + n_convert:16 + n_convert]
    conv_dst = refs[16 + n_convert:16 + 2 * n_convert]
    (v_s,) = refs[16 + 2 * n_convert:]
    shared = (pre_ref, post_ref, win_ref, bin_ref, lng_ref, lnb_ref)

    _two_streams(
        functools.partial(_sgu_tile, xp_ref, yp_ref, None, *shared, wmix_p_ref, bmix_p_ref,
                          wout_ref, v_s, L=chunk_p,
                          side_work=functools.partial(_convert, conv_src, conv_dst)),
        functools.partial(_sgu_tile, xs_ref, ys_ref, vs_ref, *shared, wmix_s_ref, bmix_s_ref,
                          wout_ref, v_s, L=chunk_s),
        n_prompt)


def _sgu(xp, xs, pre, post, w_in, b_in, ln_g, ln_b, w_s, b_s, w_out, convert, *, seq_p, seq_s):
    mp, d = xp.shape
    ms = xs.shape[0]
    half = w_out.shape[0]
    win = w_s.shape[1]
    chunk_p, chunk_s = min(seq_p, win), min(seq_s, win)
    tile = TOKEN_TILE
    assert chunk_p == win and tile % win == 0 and seq_p % win == 0
    assert ms % win == 0 and win % chunk_s == 0 and seq_s == chunk_s and ms <= tile
    n_prompt = mp // tile
    reps = win // chunk_s
    wmix_s = jnp.tile(w_s[:, :chunk_s, :chunk_s], (1, reps, reps))
    bmix_s = jnp.tile(b_s[:, :chunk_s].T, (reps, 1))
    rows_p = _prompt_rows(tile, d, n_prompt)
    c_in, c_out, c_shape = _convert_specs(convert, n_prompt)
    kern = functools.partial(_sgu_kernel, n_prompt=n_prompt, chunk_p=chunk_p, chunk_s=chunk_s,
                             n_convert=len(convert))
    out = pl.pallas_call(
        kern,
        grid=(n_prompt + 1,),
        in_specs=[rows_p, _resident((ms, d)), _resident((1, d)), _resident((1, d)),
                  _resident(w_in.shape), _resident((1, 2 * half)), _resident((1, half)),
                  _resident((1, half)), _resident(w_s.shape), _resident((win, w_s.shape[0])),
                  _resident(w_s.shape), _resident((win, w_s.shape[0])), _resident(w_out.shape),
                  *c_in],
        out_specs=[rows_p, pl.BlockSpec((ms, d), lambda i: (0, 0)),
                   pl.BlockSpec((ms, half), lambda i: (0, 0)), *c_out],
        out_shape=[jax.ShapeDtypeStruct((mp, d), F32), jax.ShapeDtypeStruct((ms, d), F32),
                   jax.ShapeDtypeStruct((ms, half), F32), *c_shape],
        scratch_shapes=[pltpu.VMEM((tile, half), F32)],
        compiler_params=_PARAMS,
        name="sgu",
    )(xp, xs, pre.reshape(1, d), post.reshape(1, d), w_in, b_in.reshape(1, -1),
      ln_g.reshape(1, -1), ln_b.reshape(1, -1), w_s, b_s.T, wmix_s, bmix_s, w_out,
      *[w for w, _ in convert])
    return out[0], out[1], out[2], out[3:]


def kernel(x_prompt, x_sample, state_gla, norm_pre, norm_post, ffn_w_gate, ffn_w_up, ffn_w_down,
           gla_w_in, gla_w_gate2, gla_b_gate, gla_norm, gla_w_out, sgu_w_in, sgu_b_in, sgu_ln_g,
           sgu_ln_b, sgu_w_s, sgu_b_s, sgu_w_out):
    batch, seq, d = x_prompt.shape
    dbatch, dseq, _ = x_sample.shape
    depth = norm_pre.shape[0]
    xp = x_prompt.reshape(batch * seq, d)
    xs = x_sample.reshape(dbatch * dseq, d)
    gla_p, gla_s, sgu_s = [], [], []

    def ffn_sources(i, k):
        return [(ffn_w_gate, (i, k)), (ffn_w_up, (i, k)), (ffn_w_down, (i, k))]

    sources = []
    for i in range(depth):
        j = i // 2
        mixer = [(gla_w_in, (j,)), (gla_w_out, (j,))] if i % 2 == 0 else \
            [(sgu_w_in, (j,)), (sgu_w_out, (j,))]
        sources += [ffn_sources(i, 0), mixer, ffn_sources(i, 1)]
    sources.append([])

    ready = None
    for i in range(depth):
        j = i // 2
        for part in range(3):
            nxt = sources[3 * i + part + 1]
            if part != 1:
                k, n = (0, 0) if part == 0 else (1, 2)
                if ready is None:
                    w, lead = (ffn_w_gate, ffn_w_up, ffn_w_down), (i, k)
                else:
                    w, lead = ready, None
                xp, xs, ready = _ffn(xp, xs, norm_pre[i, n], norm_post[i, n], *w, lead, nxt)
            elif i % 2 == 0:
                w_in, w_out = ready
                xp, xs, sp, ss, ready = _gla(
                    xp, xs, state_gla[j], norm_pre[i, 1], norm_post[i, 1], w_in, gla_w_gate2[j],
                    gla_b_gate[j], gla_norm[j], w_out, nxt, batch_p=batch, batch_s=dbatch)
                gla_p.append(sp)
                gla_s.append(ss)
            else:
                w_in, w_out = ready
                xp, xs, vs, ready = _sgu(
                    xp, xs, norm_pre[i, 1], norm_post[i, 1], w_in, sgu_b_in[j], sgu_ln_g[j],
                    sgu_ln_b[j], sgu_w_s[j], sgu_b_s[j], w_out, nxt, seq_p=seq, seq_s=dseq)
                sgu_s.append(vs.reshape(dbatch, dseq, -1))

    return (xp.reshape(batch, seq, d), xs.reshape(dbatch, dseq, d),
            jnp.stack(gla_p), jnp.stack(gla_s), jnp.stack(sgu_s))
```
